```python
import math
import jax
import jax.numpy as jnp
from jax import lax
import numpy as np

D_MODEL = 4096
BATCH = 4
SEQ = 2048
DEPTH = 4

HEAD_DIM = 128
FOX_WIDTH = D_MODEL // 4
FOX_HEADS = FOX_WIDTH // HEAD_DIM
FOX_BLOCK = 128
S5_WIDTH = D_MODEL // 4
S5_GROUP_CH = 16
S5_GROUPS = S5_WIDTH // S5_GROUP_CH
S5_STATE = 64
NSA_WIDTH = D_MODEL // 2
NSA_HEADS = NSA_WIDTH // HEAD_DIM
NSA_KV_HEADS = 4
NSA_REP = NSA_HEADS // NSA_KV_HEADS
NSA_KV_WIDTH = NSA_KV_HEADS * HEAD_DIM
CMP_BLOCK = 32
CMP_STRIDE = 16
SLC_BLOCK = 64
SLC_TOPN = 16
SLC_Q_CHUNK = 32
WINDOW = 512
WIN_BLOCK = 128
FORCE_BONUS = 1e3
MIX_WIDTH = FOX_WIDTH + S5_WIDTH + NSA_WIDTH
REL_BUCKETS = 32
REL_EXACT = 16
REL_MAX_DIST = 128
D_FF = 11008
CONV_WIDTH = 3
RMS_EPS = 1e-6
NEG_INF = -1e30
IN_SPLITS = [FOX_WIDTH, FOX_WIDTH, FOX_WIDTH, FOX_HEADS, S5_WIDTH, NSA_WIDTH,
             NSA_KV_WIDTH, NSA_KV_WIDTH, NSA_KV_WIDTH, NSA_KV_WIDTH, NSA_KV_WIDTH, NSA_KV_WIDTH,
             3 * NSA_HEADS]
N_IN = sum(IN_SPLITS)

kernel_name = 'hybrid_fox_s5_nsa_sandwich'


def rmsnorm(x, g):
    xf = x.astype(jnp.float32)
    y = xf * lax.rsqrt(jnp.mean(xf * xf, axis=-1, keepdims=True) + RMS_EPS)
    return (y * g.astype(jnp.float32)).astype(x.dtype)


def t5_bucket(dist):
    n = jnp.maximum(dist, 0)
    nf = jnp.maximum(n, 1).astype(jnp.float32)
    large = REL_EXACT + (jnp.log(nf / REL_EXACT) / math.log(REL_MAX_DIST / REL_EXACT)
                         * (REL_BUCKETS - REL_EXACT)).astype(jnp.int32)
    return jnp.where(n < REL_EXACT, n, jnp.minimum(large, REL_BUCKETS - 1))


def fox_mixer(q, k, v, z_f, b_f):
    B, S = q.shape[:2]
    q = q.reshape(B, S, FOX_HEADS, HEAD_DIM)
    k = k.reshape(B, S, FOX_HEADS, HEAD_DIM)
    v = v.reshape(B, S, FOX_HEADS, HEAD_DIM)
    log_f = jax.nn.log_sigmoid(z_f.astype(jnp.float32) + b_f.astype(jnp.float32))
    c_k = jnp.transpose(lax.cumsum(log_f, axis=1), (0, 2, 1))
    nb = S // FOX_BLOCK
    qb = jnp.moveaxis(q.reshape(B, nb, FOX_BLOCK, FOX_HEADS, HEAD_DIM), 1, 0)
    cb = jnp.moveaxis(c_k.reshape(B, FOX_HEADS, nb, FOX_BLOCK), 2, 0)
    starts = jnp.arange(nb, dtype=jnp.int32) * FOX_BLOCK
    kpos = jnp.arange(S, dtype=jnp.int32)
    scale = HEAD_DIM ** -0.5

    def block(args):
        qi, ci, st = args
        s = (jnp.einsum('bqhd,bkhd->bhqk', qi, k).astype(jnp.float32) * scale
             + ci[..., None] - c_k[:, :, None, :])
        causal = kpos[None, :] <= (st + jnp.arange(FOX_BLOCK, dtype=jnp.int32))[:, None]
        p = jax.nn.softmax(jnp.where(causal, s, NEG_INF), axis=-1)
        return jnp.einsum('bhqk,bkhd->bqhd', p.astype(v.dtype), v)

    o = lax.map(block, (qb, cb, starts))
    return jnp.moveaxis(o, 0, 1).reshape(B, S, FOX_WIDTH)


def _ssm_combine(e1, e2):
    a1r, a1i, b1r, b1i = e1
    a2r, a2i, b2r, b2i = e2
    return (a2r * a1r - a2i * a1i,
            a2r * a1i + a2i * a1r,
            a2r * b1r - a2i * b1i + b2r,
            a2r * b1i + a2i * b1r + b2i)


def s5_mixer(u, a_re, a_im, log_dt, b_re, b_im, c_re, c_im, d_skip, w_glu):
    B, S = u.shape[:2]
    f32 = jnp.float32
    uf = u.astype(f32).reshape(B, S, S5_GROUPS, S5_GROUP_CH)
    lam_re = jnp.minimum(a_re.astype(f32), -1e-4)
    lam_im = a_im.astype(f32)
    dt = jnp.exp(log_dt.astype(f32))[:, None]
    mag = jnp.exp(lam_re * dt)
    ang = lam_im * dt
    lb_re, lb_im = mag * jnp.cos(ang), mag * jnp.sin(ang)
    den = lam_re * lam_re + lam_im * lam_im
    nr, ni = lb_re - 1.0, lb_im
    coef_re = (nr * lam_re + ni * lam_im) / den
    coef_im = (ni * lam_re - nr * lam_im) / den
    b_re, b_im = b_re.astype(f32), b_im.astype(f32)
    bb_re = coef_re[..., None] * b_re - coef_im[..., None] * b_im
    bb_im = coef_re[..., None] * b_im + coef_im[..., None] * b_re
    bu_re = jnp.einsum('bsgh,gph->bsgp', uf, bb_re)
    bu_im = jnp.einsum('bsgh,gph->bsgp', uf, bb_im)
    a_re_t = jnp.broadcast_to(lb_re, bu_re.shape)
    a_im_t = jnp.broadcast_to(lb_im, bu_re.shape)
    _, _, h_re, h_im = lax.associative_scan(_ssm_combine, (a_re_t, a_im_t, bu_re, bu_im), axis=1)
    y = (jnp.einsum('ghp,bsgp->bsgh', c_re.astype(f32), h_re)
         - jnp.einsum('ghp,bsgp->bsgh', c_im.astype(f32), h_im)
         + d_skip.astype(f32).reshape(S5_GROUPS, S5_GROUP_CH) * uf)
    y = jax.nn.gelu(y.reshape(B, S, S5_WIDTH))
    return y * jax.nn.sigmoid(y @ w_glu.astype(f32))


def compress(kv, pe, w):
    B, S = kv.shape[:2]
    ch = kv.reshape(B, S // CMP_STRIDE, CMP_STRIDE, NSA_KV_HEADS, HEAD_DIM)
    blk = jnp.concatenate([ch[:, :-1], ch[:, 1:]], axis=2)
    return jnp.einsum('bnlgd,lde->bnge', blk + pe[:, None, :], w)


def nsa_mixer(q, k_c, v_c, k_s, v_s, k_w, v_w, z_gate, pe_k, pe_v, w_ck, w_cv, rel_bias):
    B, S = q.shape[:2]
    G, R, Dh = NSA_KV_HEADS, NSA_REP, HEAD_DIM
    scale = Dh ** -0.5
    q = q.reshape(B, S, G, R, Dh)
    kv_shape = (B, S, G, Dh)
    k_c, v_c, k_s, v_s, k_w, v_w = [t.reshape(kv_shape) for t in (k_c, v_c, k_s, v_s, k_w, v_w)]

    kc = compress(k_c, pe_k, w_ck)
    vc = compress(v_c, pe_v, w_cv)
    nc = kc.shape[1]
    blk_start = jnp.arange(nc, dtype=jnp.int32) * CMP_STRIDE
    blk_end = blk_start + CMP_BLOCK - 1
    ns = S // SLC_BLOCK
    n_sel = min(SLC_TOPN, ns)
    sel_start = jnp.arange(ns, dtype=jnp.int32) * SLC_BLOCK
    overlap = ((blk_start[:, None] <= sel_start[None, :] + SLC_BLOCK - 1)
               & (blk_end[:, None] >= sel_start[None, :])).astype(jnp.float32)
    kb = jnp.transpose(k_s.reshape(B, ns, SLC_BLOCK, G, Dh), (0, 3, 1, 2, 4))
    vb = jnp.transpose(v_s.reshape(B, ns, SLC_BLOCK, G, Dh), (0, 3, 1, 2, 4))
    bias_gr = rel_bias.reshape(REL_BUCKETS, G, R)
    bix = jnp.arange(B)[:, None, None, None]
    gix = jnp.arange(G)[None, None, :, None]
    jb = jnp.arange(ns, dtype=jnp.int32)

    def cmp_slc_chunk(args):
        qi, st = args
        tq = st + jnp.arange(SLC_Q_CHUNK, dtype=jnp.int32)
        s = jnp.einsum('bqgrd,bngd->bqgrn', qi, kc).astype(jnp.float32) * scale
        bias_c = rel_bias[t5_bucket(tq[:, None] - blk_end[None, :])]
        bias_c = jnp.transpose(bias_c.reshape(SLC_Q_CHUNK, nc, G, R), (0, 2, 3, 1))
        valid_c = (blk_end[None, :] <= tq[:, None])[:, None, None, :]
        p = jnp.where(valid_c, jax.nn.softmax(jnp.where(valid_c, s + bias_c, NEG_INF), axis=-1), 0.0)
        o_cmp = jnp.einsum('bqgrn,bngd->bqgrd', p.astype(vc.dtype), vc)
        imp = jnp.einsum('bqgn,nj->bqgj', p.sum(axis=3), overlap)
        cur = (tq // SLC_BLOCK)[:, None]
        causal_b = jb[None, :] * SLC_BLOCK <= tq[:, None]
        forced = ((jb[None, :] == 0) | (jb[None, :] == cur) | (jb[None, :] == cur - 1)).astype(jnp.float32)
        score = jnp.where(causal_b[None, :, None, :], imp + FORCE_BONUS * forced[None, :, None, :], NEG_INF)
        _, idx = lax.top_k(score, n_sel)
        kg = kb[bix, gix, idx]
        vg = vb[bix, gix, idx]
        kpos = idx[..., None] * SLC_BLOCK + jnp.arange(SLC_BLOCK, dtype=jnp.int32)
        dist = tq[None, :, None, None, None] - kpos
        bias_s = jnp.moveaxis(bias_gr[t5_bucket(dist), gix[..., None]], -1, 3)
        s2 = jnp.einsum('bqgrd,bqgnkd->bqgrnk', qi, kg).astype(jnp.float32) * scale + bias_s
        s2 = jnp.where((dist >= 0)[:, :, :, None], s2, NEG_INF)
        p2 = jax.nn.softmax(s2.reshape(s2.shape[:4] + (n_sel * SLC_BLOCK,)), axis=-1)
        o_slc = jnp.einsum('bqgrm,bqgmd->bqgrd', p2.astype(vg.dtype),
                           vg.reshape(vg.shape[:3] + (n_sel * SLC_BLOCK, Dh)))
        return o_cmp, o_slc

    nq = S // SLC_Q_CHUNK
    q_chunks = jnp.moveaxis(q.reshape(B, nq, SLC_Q_CHUNK, G, R, Dh), 1, 0)
    o_cmp, o_slc = lax.map(cmp_slc_chunk, (q_chunks, jnp.arange(nq, dtype=jnp.int32) * SLC_Q_CHUNK))
    o_cmp = jnp.moveaxis(o_cmp, 0, 1).reshape(B, S, G, R, Dh)
    o_slc = jnp.moveaxis(o_slc, 0, 1).reshape(B, S, G, R, Dh)

    kw_pad = jnp.pad(k_w, ((0, 0), (WINDOW, 0), (0, 0), (0, 0)))
    vw_pad = jnp.pad(v_w, ((0, 0), (WINDOW, 0), (0, 0), (0, 0)))
    span = WIN_BLOCK + WINDOW
    wdist = (jnp.arange(WIN_BLOCK, dtype=jnp.int32)[:, None]
             - jnp.arange(span, dtype=jnp.int32)[None, :] + WINDOW)
    band = (wdist >= 0) & (wdist < WINDOW)
    bias_w = jnp.transpose(rel_bias[t5_bucket(wdist)].reshape(WIN_BLOCK, span, G, R), (0, 2, 3, 1))

    def win_block(args):
        qi, st = args
        kw = lax.dynamic_slice_in_dim(kw_pad, st, span, axis=1)
        vw = lax.dynamic_slice_in_dim(vw_pad, st, span, axis=1)
        s = jnp.einsum('bqgrd,bkgd->bqgrk', qi, kw).astype(jnp.float32) * scale + bias_w
        keypos = st - WINDOW + jnp.arange(span, dtype=jnp.int32)
        mask = (band & (keypos >= 0)[None, :])[:, None, None, :]
        p = jax.nn.softmax(jnp.where(mask, s, NEG_INF), axis=-1)
        return jnp.einsum('bqgrk,bkgd->bqgrd', p.astype(vw.dtype), vw)

    nw = S // WIN_BLOCK
    o_win = lax.map(win_block, (jnp.moveaxis(q.reshape(B, nw, WIN_BLOCK, G, R, Dh), 1, 0),
                                jnp.arange(nw, dtype=jnp.int32) * WIN_BLOCK))
    o_win = jnp.moveaxis(o_win, 0, 1).reshape(B, S, G, R, Dh)

    gates = jax.nn.sigmoid(z_gate.astype(jnp.float32)).reshape(B, S, G, R, 3)
    o = gates[..., 0:1] * o_cmp + gates[..., 1:2] * o_slc + gates[..., 2:3] * o_win
    return o.reshape(B, S, NSA_WIDTH)


def conv_ffn(h, w_up, conv_w, conv_b, w_down):
    S = h.shape[1]
    u = h @ w_up
    up = jnp.pad(u, ((0, 0), (CONV_WIDTH - 1, 0), (0, 0)))
    y = conv_b
    for tap in range(CONV_WIDTH):
        y = y + conv_w[tap] * up[:, tap:tap + S]
    gate, val = jnp.split(y, 2, axis=-1)
    return (jax.nn.gelu(gate, approximate=True) * val) @ w_down


def setup_inputs(seed: int = 0) -> dict:
    key = jax.random.key(seed)
    ks = jax.random.split(key, 32)

    def nrm(k, shape, scale):
        return jax.random.normal(k, shape, jnp.float32) * scale

    def gain(k, shape):
        return 1.0 + nrm(k, shape, 0.05)

    L, G, P, H = DEPTH, S5_GROUPS, S5_STATE, S5_GROUP_CH
    n_idx = jnp.arange(P, dtype=jnp.float32)
    return {
        'x': nrm(ks[0], (BATCH, SEQ, D_MODEL), 1.0),
        'w_in': nrm(ks[1], (L, D_MODEL, N_IN), D_MODEL ** -0.5),
        'b_forget': jax.random.uniform(ks[2], (L, FOX_HEADS), jnp.float32, 1.0, 4.0),
        's5_a_re': -0.5 + nrm(ks[3], (L, G, P), 0.01),
        's5_a_im': math.pi * n_idx + nrm(ks[4], (L, G, P), 0.01),
        's5_log_dt': jax.random.uniform(ks[5], (L, G), jnp.float32, math.log(1e-3), math.log(1e-1)),
        's5_b_re': nrm(ks[6], (L, G, P, H), (2 * H) ** -0.5),
        's5_b_im': nrm(ks[7], (L, G, P, H), (2 * H) ** -0.5),
        's5_c_re': nrm(ks[8], (L, G, H, P), 1.0),
        's5_c_im': nrm(ks[9], (L, G, H, P), 1.0),
        's5_d': nrm(ks[10], (L, S5_WIDTH), 0.5),
        's5_w_glu': nrm(ks[11], (L, S5_WIDTH, S5_WIDTH), S5_WIDTH ** -0.5),
        'cmp_pe_k': nrm(ks[12], (L, CMP_BLOCK, HEAD_DIM), 0.5),
        'cmp_pe_v': nrm(ks[13], (L, CMP_BLOCK, HEAD_DIM), 0.5),
        'cmp_w_k': nrm(ks[14], (L, CMP_BLOCK, HEAD_DIM, HEAD_DIM), (CMP_BLOCK * HEAD_DIM) ** -0.5),
        'cmp_w_v': nrm(ks[15], (L, CMP_BLOCK, HEAD_DIM, HEAD_DIM), (CMP_BLOCK * HEAD_DIM) ** -0.5),
        'rel_bias': nrm(ks[16], (REL_BUCKETS, NSA_HEADS), 0.5),
        'g_out_fox': gain(ks[17], (L, FOX_WIDTH)),
        'g_out_s5': gain(ks[18], (L, S5_WIDTH)),
        'g_out_nsa': gain(ks[19], (L, NSA_WIDTH)),
        'w_out': nrm(ks[20], (L, MIX_WIDTH, D_MODEL), MIX_WIDTH ** -0.5),
        'g_pre_mix': gain(ks[21], (L, D_MODEL)),
        'g_post_mix': gain(ks[22], (L, D_MODEL)),
        'g_pre_ffn': gain(ks[23], (L, D_MODEL)),
        'g_post_ffn': gain(ks[24], (L, D_MODEL)),
        'w_up': nrm(ks[25], (L, D_MODEL, 2 * D_FF), D_MODEL ** -0.5),
        'conv_w': nrm(ks[26], (L, CONV_WIDTH, 2 * D_FF), CONV_WIDTH ** -0.5),
        'conv_b': nrm(ks[27], (L, 2 * D_FF), 0.02),
        'w_down': nrm(ks[28], (L, D_FF, D_MODEL), D_FF ** -0.5),
    }


def reference(x, w_in, b_forget, s5_a_re, s5_a_im, s5_log_dt, s5_b_re, s5_b_im, s5_c_re, s5_c_im,
              s5_d, s5_w_glu, cmp_pe_k, cmp_pe_v, cmp_w_k, cmp_w_v, rel_bias, g_out_fox, g_out_s5,
              g_out_nsa, w_out, g_pre_mix, g_post_mix, g_pre_ffn, g_post_ffn, w_up, conv_w, conv_b,
              w_down):
    split_points = np.cumsum(IN_SPLITS)[:-1].tolist()
    for l in range(DEPTH):
        h = rmsnorm(x, g_pre_mix[l])
        z = h @ w_in[l]
        (q_a, k_a, v_a, z_f, u_b, q_c, k_cc, v_cc, k_cs, v_cs, k_cw, v_cw,
         z_g) = jnp.split(z, split_points, axis=-1)
        o_a = fox_mixer(q_a, k_a, v_a, z_f, b_forget[l])
        o_b = s5_mixer(u_b, s5_a_re[l], s5_a_im[l], s5_log_dt[l], s5_b_re[l], s5_b_im[l],
                       s5_c_re[l], s5_c_im[l], s5_d[l], s5_w_glu[l])
        o_c = nsa_mixer(q_c, k_cc, v_cc, k_cs, v_cs, k_cw, v_cw, z_g, cmp_pe_k[l], cmp_pe_v[l],
                        cmp_w_k[l], cmp_w_v[l], rel_bias)
        mixed = jnp.concatenate([rmsnorm(o_a.astype(x.dtype), g_out_fox[l]),
                                 rmsnorm(o_b.astype(x.dtype), g_out_s5[l]),
                                 rmsnorm(o_c.astype(x.dtype), g_out_nsa[l])], axis=-1) @ w_out[l]
        x = x + rmsnorm(mixed, g_post_mix[l])
        f = conv_ffn(rmsnorm(x, g_pre_ffn[l]), w_up[l], conv_w[l], conv_b[l], w_down[l])
        x = x + rmsnorm(f.astype(x.dtype), g_post_ffn[l])
    return x
```

```python
import functools
import math

import numpy as np
import jax
import jax.numpy as jnp
from jax import lax
from jax.experimental import pallas as pl
from jax.experimental.pallas import tpu as pltpu

D_MODEL = 4096
DEPTH = 4
HEAD_DIM = 128
FOX_WIDTH = 1024
FOX_HEADS = 8
S5_WIDTH = 1024
S5_GROUP_CH = 16
S5_GROUPS = 64
S5_STATE = 64
NSA_WIDTH = 2048
NSA_HEADS = 16
NSA_KV_HEADS = 4
NSA_REP = 4
NSA_KV_WIDTH = 512
CMP_BLOCK = 32
CMP_STRIDE = 16
SLC_BLOCK = 64
SLC_TOPN = 16
WINDOW = 512
FORCE_BONUS = 1e3
REL_BUCKETS = 32
REL_EXACT = 16
REL_MAX_DIST = 128
D_FF = 11008
RMS_EPS = 1e-6
NEG_INF = -1e30
ATTN_SCALE = HEAD_DIM ** -0.5

LANES = 128
SUBLANES = 8
VMEM_LIMIT = 56 * 1024 * 1024

D_FF_PAD = 11264
S5_SLAB = 256
S5_SLAB_STATE = S5_SLAB // S5_GROUP_CH * S5_STATE
N_S5_SLABS = S5_WIDTH // S5_SLAB
COL_QA, COL_KA, COL_VA, COL_UB, COL_QC = 0, 1024, 2048, 3072, 4096
COL_KCC, COL_VCC, COL_KCS, COL_VCS, COL_KCW, COL_VCW = 6144, 6656, 7168, 7680, 8192, 8704
N_MAIN = 9216
N_GATE = LANES * (1 + NSA_KV_HEADS)
RESID_COLS = 512
RESID_ROWS = 128
TQ = 128
N_CMP_PAD = 128
N_SLC = 32


def _cparams(*sem):
    return pltpu.CompilerParams(dimension_semantics=sem, vmem_limit_bytes=VMEM_LIMIT)


def _bf16(x):
    return x.astype(jnp.bfloat16)


def _dot(a, b):
    return jnp.dot(a, b, preferred_element_type=jnp.float32)


def _dot_nt(a, b):
    return lax.dot_general(a, b, (((1,), (1,)), ((), ())), preferred_element_type=jnp.float32)


def _rms_scale(x):
    return lax.rsqrt(jnp.mean(x * x, axis=-1, keepdims=True) + RMS_EPS)


def _gelu_tanh(x):
    return 0.5 * x * (1.0 + jnp.tanh(math.sqrt(2.0 / math.pi) * (x + 0.044715 * (x * x * x))))


def _sigmoid(x):
    return 1.0 / (1.0 + jnp.exp(-x))


def _rmsnorm_cast_kernel(x_ref, g_ref, o_ref):
    x = x_ref[...]
    o_ref[...] = _bf16(x * _rms_scale(x) * g_ref[...])


def _rmsnorm_cast(x, g, tm=512):
    t, d = x.shape
    return pl.pallas_call(
        _rmsnorm_cast_kernel,
        grid=(t // tm,),
        in_specs=[pl.BlockSpec((tm, d), lambda i: (i, 0)), pl.BlockSpec((1, d), lambda i: (0, 0))],
        out_specs=pl.BlockSpec((tm, d), lambda i: (i, 0)),
        out_shape=jax.ShapeDtypeStruct((t, d), jnp.bfloat16),
        compiler_params=_cparams("parallel"),
        name="rmsnorm_cast",
    )(x, g.reshape(1, d))


def _matmul_kernel(a_ref, w_ref, o_ref):
    o_ref[...] = _dot(a_ref[...], w_ref[...]).astype(o_ref.dtype)


def _matmul(a, w, tm, tn, out_dtype, name):
    t, k = a.shape
    n = w.shape[1]
    return pl.pallas_call(
        _matmul_kernel,
        grid=(t // tm, n // tn),
        in_specs=[pl.BlockSpec((tm, k), lambda i, j: (i, 0)), pl.BlockSpec((k, tn), lambda i, j: (0, j))],
        out_specs=pl.BlockSpec((tm, tn), lambda i, j: (i, j)),
        out_shape=jax.ShapeDtypeStruct((t, n), out_dtype),
        compiler_params=_cparams("parallel", "arbitrary"),
        name=name,
    )(a, w)


def _matmul_resid_kernel(a_ref, w_ref, x_ref, gpost_ref, gnext_ref, xo_ref, hn_ref):
    k = pl.program_id(1)
    tm, d = xo_ref.shape
    a = a_ref[...]

    @pl.when(k == 0)
    def _():
        xo_ref[...] = jnp.zeros_like(xo_ref)

    for n in range(d // RESID_COLS):
        cols = slice(n * RESID_COLS, (n + 1) * RESID_COLS)
        xo_ref[:, cols] += _dot(a, w_ref[:, cols])

    @pl.when(k == pl.num_programs(1) - 1)
    def _():
        for r in range(tm // RESID_ROWS):
            rows = slice(r * RESID_ROWS, (r + 1) * RESID_ROWS)
            f = xo_ref[rows, :]
            xn = x_ref[rows, :] + f * _rms_scale(f) * gpost_ref[...]
            xo_ref[rows, :] = xn
            hn_ref[rows, :] = _bf16(xn * _rms_scale(xn) * gnext_ref[...])


def _matmul_resid(a, w, x, g_post, g_next, tm, tk, name):
    t, kk = a.shape
    d = w.shape[1]
    return pl.pallas_call(
        _matmul_resid_kernel,
        grid=(t // tm, kk // tk),
        in_specs=[
            pl.BlockSpec((tm, tk), lambda i, k: (i, k)),
            pl.BlockSpec((tk, d), lambda i, k: (k, 0)),
            pl.BlockSpec((tm, d), lambda i, k: (i, 0), pipeline_mode=pl.Buffered(1)),
            pl.BlockSpec((1, d), lambda i, k: (0, 0)),
            pl.BlockSpec((1, d), lambda i, k: (0, 0)),
        ],
        out_specs=[pl.BlockSpec((tm, d), lambda i, k: (i, 0)), pl.BlockSpec((tm, d), lambda i, k: (i, 0))],
        out_shape=[jax.ShapeDtypeStruct((t, d), jnp.float32), jax.ShapeDtypeStruct((t, d), jnp.bfloat16)],
        compiler_params=_cparams("parallel", "arbitrary"),
        name=name,
    )(a, w, x, g_post.reshape(1, d), g_next.reshape(1, d))


def _ffn_up_kernel(seq, h_ref, halo_ref, wg_ref, wv_ref, cwg_ref, cwv_ref, cbg_ref, cbv_ref, o_ref):
    i = pl.program_id(0)
    tm = h_ref.shape[0]
    h = h_ref[...]
    keep = ((i * tm) % seq != 0).astype(jnp.float32)
    halo = halo_ref[...]
    row = lax.broadcasted_iota(jnp.int32, (tm, 1), 0)

    def conv(w_ref, cw_ref, cb_ref):
        w = w_ref[...]
        u = _dot(h, w)
        uh = _dot(halo, w) * keep
        p1 = jnp.where(row == 0, uh[7:8], pltpu.roll(u, 1, 0))
        p2 = jnp.where(row == 0, uh[6:7], jnp.where(row == 1, uh[7:8], pltpu.roll(u, 2, 0)))
        cw = cw_ref[...]
        return cb_ref[...] + cw[0:1] * p2 + cw[1:2] * p1 + cw[2:3] * u

    gate = conv(wg_ref, cwg_ref, cbg_ref)
    val = conv(wv_ref, cwv_ref, cbv_ref)
    o_ref[...] = _bf16(_gelu_tanh(gate) * val)


def _ffn_up(hn, wg, wv, cwg, cwv, cbg, cbv, seq, tm=512, tn=512):
    t, d = hn.shape
    f = wg.shape[1]
    hb = tm // SUBLANES
    return pl.pallas_call(
        functools.partial(_ffn_up_kernel, seq),
        grid=(t // tm, f // tn),
        in_specs=[
            pl.BlockSpec((tm, d), lambda i, j: (i, 0)),
            pl.BlockSpec((SUBLANES, d), lambda i, j: (jnp.maximum(i * hb - 1, 0), 0)),
            pl.BlockSpec((d, tn), lambda i, j: (0, j)),
            pl.BlockSpec((d, tn), lambda i, j: (0, j)),
            pl.BlockSpec((3, tn), lambda i, j: (0, j)),
            pl.BlockSpec((3, tn), lambda i, j: (0, j)),
            pl.BlockSpec((1, tn), lambda i, j: (0, j)),
            pl.BlockSpec((1, tn), lambda i, j: (0, j)),
        ],
        out_specs=pl.BlockSpec((tm, tn), lambda i, j: (i, j)),
        out_shape=jax.ShapeDtypeStruct((t, f), jnp.bfloat16),
        compiler_params=_cparams("parallel", "arbitrary"),
        name="ffn_up_conv_geglu",
    )(hn, hn, wg, wv, cwg, cwv, cbg, cbv)


def _split3(x):
    hi = _bf16(x)
    r = x - hi.astype(jnp.float32)
    mid = _bf16(r)
    lo = _bf16(r - mid.astype(jnp.float32))
    return hi, mid, lo


def _fox_gate_kernel(zf_ref, bf_ref, ccol_ref, crow_ref):
    seq = zf_ref.shape[0]
    blk = 256
    x = zf_ref[...] + bf_ref[...]
    logf = -(jnp.maximum(-x, 0.0) + jnp.log(1.0 + jnp.exp(-jnp.abs(x))))
    r = lax.broadcasted_iota(jnp.int32, (blk, blk), 0)
    c = lax.broadcasted_iota(jnp.int32, (blk, blk), 1)
    tri = _bf16((c <= r).astype(jnp.float32))
    carry = jnp.zeros((1, LANES), jnp.float32)
    for b in range(seq // blk):
        hi, mid, lo = _split3(logf[b * blk:(b + 1) * blk])
        cs = _dot(tri, hi) + _dot(tri, mid) + _dot(tri, lo) + carry
        ccol_ref[b * blk:(b + 1) * blk, :] = cs
        carry = cs[blk - 1:blk]
    crow_ref[...] = ccol_ref[...].T


def _fox_gate(z_gate, b_f, batch, seq):
    bf = jnp.zeros((1, LANES), jnp.float32).at[0, :FOX_HEADS].set(b_f)
    return pl.pallas_call(
        _fox_gate_kernel,
        grid=(batch,),
        in_specs=[pl.BlockSpec((seq, LANES), lambda b: (b, 0)), pl.BlockSpec((1, LANES), lambda b: (0, 0))],
        out_specs=[pl.BlockSpec((seq, LANES), lambda b: (b, 0)), pl.BlockSpec((None, LANES, seq), lambda b: (b, 0, 0))],
        out_shape=[jax.ShapeDtypeStruct((batch * seq, LANES), jnp.float32),
                   jax.ShapeDtypeStruct((batch, LANES, seq), jnp.float32)],
        compiler_params=_cparams("parallel"),
        name="fox_gate_cumsum",
    )(z_gate, bf)


def _fox_attn_kernel(q_ref, k_ref, v_ref, cq_ref, ck_ref, o_ref):
    qi = pl.program_id(2)
    tq = q_ref.shape[0]
    tk = LANES
    qb = _bf16(q_ref[...])
    cq = cq_ref[...]
    tpos = qi * tq + lax.broadcasted_iota(jnp.int32, (tq, tk), 0)
    lane = lax.broadcasted_iota(jnp.int32, (tq, tk), 1)

    def body(kt, carry):
        m, l, acc = carry
        off = pl.multiple_of(kt * tk, tk)
        kb = _bf16(k_ref[pl.ds(off, tk), :])
        vb = _bf16(v_ref[pl.ds(off, tk), :])
        s = _dot_nt(qb, kb) * ATTN_SCALE + cq - ck_ref[:, pl.ds(off, tk)]
        s = jnp.where(off + lane <= tpos, s, NEG_INF)
        m_new = jnp.maximum(m, jnp.max(s, axis=-1, keepdims=True))
        alpha = jnp.exp(m - m_new)
        p = jnp.exp(s - m_new)
        l = alpha * l + jnp.sum(p, axis=-1, keepdims=True)
        acc = alpha * acc + _dot(_bf16(p), vb)
        return m_new, l, acc

    init = (jnp.full((tq, 1), -jnp.inf, jnp.float32), jnp.zeros((tq, 1), jnp.float32),
            jnp.zeros((tq, HEAD_DIM), jnp.float32))
    _, l, acc = lax.fori_loop(0, (qi + 1) * (tq // tk), body, init)
    o_ref[...] = acc / l


def _fox_attn(z, ccol, crow, batch, seq, tq=512):
    nq = seq // tq
    cq = ccol[:, :FOX_HEADS].reshape(batch, seq, FOX_HEADS).transpose(0, 2, 1).reshape(batch, FOX_HEADS, seq, 1)
    ck = crow[:, :FOX_HEADS, :].reshape(batch, FOX_HEADS, 1, seq)
    cb = lambda c: c // HEAD_DIM
    return pl.pallas_call(
        _fox_attn_kernel,
        grid=(batch, FOX_HEADS, nq),
        in_specs=[
            pl.BlockSpec((tq, HEAD_DIM), lambda b, h, i: (b * nq + i, cb(COL_QA) + h)),
            pl.BlockSpec((seq, HEAD_DIM), lambda b, h, i: (b, cb(COL_KA) + h)),
            pl.BlockSpec((seq, HEAD_DIM), lambda b, h, i: (b, cb(COL_VA) + h)),
            pl.BlockSpec((None, None, tq, 1), lambda b, h, i: (b, h, i, 0)),
            pl.BlockSpec((None, None, 1, seq), lambda b, h, i: (b, h, 0, 0)),
        ],
        out_specs=pl.BlockSpec((tq, HEAD_DIM), lambda b, h, i: (b * nq + i, h)),
        out_shape=jax.ShapeDtypeStruct((batch * seq, FOX_WIDTH), jnp.float32),
        compiler_params=_cparams("parallel", "parallel", "arbitrary"),
        name="fox_attention",
    )(z, z, z, cq, ck)


def _s5_disc_kernel(are_ref, aim_ref, ldt_ref, bre_ref, bim_ref, bbre_ref, bbim_ref, pwre_ref, pwim_ref):
    lam_re = jnp.minimum(are_ref[...], -1e-4)
    lam_im = aim_ref[...]
    dt = jnp.exp(ldt_ref[...])
    mag = jnp.exp(lam_re * dt)
    ang = lam_im * dt
    lb_re, lb_im = mag * jnp.cos(ang), mag * jnp.sin(ang)
    den = lam_re * lam_re + lam_im * lam_im
    nr, ni = lb_re - 1.0, lb_im
    coef_re = (nr * lam_re + ni * lam_im) / den
    coef_im = (ni * lam_re - nr * lam_im) / den
    b_re, b_im = bre_ref[...], bim_ref[...]
    bbre_ref[...] = coef_re * b_re - coef_im * b_im
    bbim_ref[...] = coef_re * b_im + coef_im * b_re
    p_re, p_im = lb_re, lb_im
    for i in range(SUBLANES):
        pwre_ref[i:i + 1, :] = p_re
        pwim_ref[i:i + 1, :] = p_im
        p_re, p_im = p_re * lb_re - p_im * lb_im, p_re * lb_im + p_im * lb_re


def _s5_discretize(a_re, a_im, log_dt, b_re, b_im):
    l, g, p = a_re.shape
    h = b_re.shape[-1]
    n = l * g * p
    row = lambda a: a.reshape(1, n)
    ldt = jnp.broadcast_to(log_dt[:, :, None], (l, g, p))
    bt = lambda b: b.reshape(n, h).T
    full = lambda r: pl.BlockSpec((r, n), lambda: (0, 0))
    bbre, bbim, pwre, pwim = pl.pallas_call(
        _s5_disc_kernel,
        in_specs=[full(1), full(1), full(1), full(h), full(h)],
        out_specs=[full(h), full(h), full(SUBLANES), full(SUBLANES)],
        out_shape=[jax.ShapeDtypeStruct((h, n), jnp.float32)] * 2 + [jax.ShapeDtypeStruct((SUBLANES, n), jnp.float32)] * 2,
        compiler_params=pltpu.CompilerParams(vmem_limit_bytes=VMEM_LIMIT),
        name="s5_discretize",
    )(row(a_re), row(a_im), row(ldt), bt(b_re), bt(b_im))
    unb = lambda b: b.T.reshape(l, g, p, h)
    unp = lambda q: q.reshape(SUBLANES, l, g * p).transpose(1, 0, 2)
    return unb(bbre), unb(bbim), unp(pwre), unp(pwim)


def _s5_scan_kernel(u_ref, bre_ref, bim_ref, cre_ref, cim_ref, d_ref, pwre_ref, pwim_ref, y_ref, hre_ref, him_ref):
    seq = u_ref.shape[0]
    u = u_ref[...]
    ub = _bf16(u)
    hre_ref[...] = _dot(ub, bre_ref[...])
    him_ref[...] = _dot(ub, bim_ref[...])

    pw_re, pw_im = pwre_ref[...], pwim_ref[...]
    sub = lax.broadcasted_iota(jnp.int32, pw_re.shape, 0)

    def shifted(i, k):
        m = sub >= k
        return (jnp.where(m, pw_re[i:i + 1], 0.0), jnp.where(m, pw_im[i:i + 1], 0.0))

    steps = [(1,) + shifted(0, 1), (2,) + shifted(1, 2), (4,) + shifted(3, 4)]

    def tile(j, carry):
        c_re, c_im = carry
        off = pl.multiple_of(j * SUBLANES, SUBLANES)
        x_re = hre_ref[pl.ds(off, SUBLANES), :]
        x_im = him_ref[pl.ds(off, SUBLANES), :]
        for k, a_re, a_im in steps:
            r_re, r_im = pltpu.roll(x_re, k, 0), pltpu.roll(x_im, k, 0)
            x_re, x_im = (x_re + a_re * r_re - a_im * r_im, x_im + a_re * r_im + a_im * r_re)
        x_re, x_im = (x_re + pw_re * c_re - pw_im * c_im, x_im + pw_re * c_im + pw_im * c_re)
        hre_ref[pl.ds(off, SUBLANES), :] = x_re
        him_ref[pl.ds(off, SUBLANES), :] = x_im
        return x_re[SUBLANES - 1:SUBLANES], x_im[SUBLANES - 1:SUBLANES]

    zero = jnp.zeros((1, pw_re.shape[1]), jnp.float32)
    lax.fori_loop(0, seq // SUBLANES, tile, (zero, zero))

    y = _dot(_bf16(hre_ref[...]), cre_ref[...]) - _dot(_bf16(him_ref[...]), cim_ref[...]) + d_ref[...] * u
    y_ref[...] = _gelu_tanh(y)


def _s5_scan(z, b_big_re, b_big_im, c_big_re, c_big_im, d_skip, pw_re, pw_im, batch, seq):
    w = S5_SLAB_STATE
    return pl.pallas_call(
        _s5_scan_kernel,
        grid=(batch, N_S5_SLABS),
        in_specs=[
            pl.BlockSpec((seq, S5_SLAB), lambda b, j: (b, COL_UB // S5_SLAB + j)),
            pl.BlockSpec((None, S5_SLAB, w), lambda b, j: (j, 0, 0)),
            pl.BlockSpec((None, S5_SLAB, w), lambda b, j: (j, 0, 0)),
            pl.BlockSpec((None, w, S5_SLAB), lambda b, j: (j, 0, 0)),
            pl.BlockSpec((None, w, S5_SLAB), lambda b, j: (j, 0, 0)),
            pl.BlockSpec((1, S5_SLAB), lambda b, j: (0, j)),
            pl.BlockSpec((SUBLANES, w), lambda b, j: (0, j)),
            pl.BlockSpec((SUBLANES, w), lambda b, j: (0, j)),
        ],
        out_specs=pl.BlockSpec((seq, S5_SLAB), lambda b, j: (b, j)),
        out_shape=jax.ShapeDtypeStruct((batch * seq, S5_WIDTH), jnp.float32),
        scratch_shapes=[pltpu.VMEM((seq, w), jnp.float32), pltpu.VMEM((seq, w), jnp.float32)],
        compiler_params=_cparams("parallel", "arbitrary"),
        name="s5_scan",
    )(z, b_big_re, b_big_im, c_big_re, c_big_im, d_skip.reshape(1, S5_WIDTH), pw_re, pw_im)


def _s5_glu_kernel(y_ref, w_ref, g_ref, o_ref):
    y = y_ref[...]
    o = y * _sigmoid(_dot(_bf16(y), w_ref[...]))
    o_ref[...] = _bf16(o * _rms_scale(o) * g_ref[...])


def _s5_glu(y, w_glu, g_out, tm=512):
    t, d = y.shape
    return pl.pallas_call(
        _s5_glu_kernel,
        grid=(t // tm,),
        in_specs=[pl.BlockSpec((tm, d), lambda i: (i, 0)), pl.BlockSpec((d, d), lambda i: (0, 0)),
                  pl.BlockSpec((1, d), lambda i: (0, 0))],
        out_specs=pl.BlockSpec((tm, d), lambda i: (i, 0)),
        out_shape=jax.ShapeDtypeStruct((t, d), jnp.bfloat16),
        compiler_params=_cparams("parallel"),
        name="s5_glu_norm",
    )(y, w_glu, g_out.reshape(1, d))


def _t5_bucket(dist):
    n = jnp.maximum(dist, 0)
    nf = jnp.maximum(n, 1).astype(jnp.float32)
    large = REL_EXACT + (jnp.log(nf / REL_EXACT) / math.log(REL_MAX_DIST / REL_EXACT)
                         * (REL_BUCKETS - REL_EXACT)).astype(jnp.int32)
    return jnp.where(n < REL_EXACT, n, jnp.minimum(large, REL_BUCKETS - 1))


def _lookup(rel_ref, bucket, h):
    out = jnp.zeros(bucket.shape, jnp.float32)
    for b in range(REL_BUCKETS):
        out = jnp.where(bucket == b, rel_ref[b, h], out)
    return out


def _bias_cmp_kernel(rel_ref, o_ref):
    qi = pl.program_id(0)
    t = qi * TQ + lax.broadcasted_iota(jnp.int32, (TQ, N_CMP_PAD), 0)
    n = lax.broadcasted_iota(jnp.int32, (TQ, N_CMP_PAD), 1)
    dist = t - (n * CMP_STRIDE + CMP_BLOCK - 1)
    valid = (dist >= 0) & (n < N_CMP_PAD - 1)
    bucket = _t5_bucket(dist)
    for h in range(NSA_HEADS):
        o_ref[h] = jnp.where(valid, _lookup(rel_ref, bucket, h), NEG_INF)


def _bias_tile_kernel(rel_ref, o_ref):
    i = lax.broadcasted_iota(jnp.int32, (TQ, TQ), 0)
    j = lax.broadcasted_iota(jnp.int32, (TQ, TQ), 1)
    b0 = _t5_bucket(i - j)
    b1 = _t5_bucket(i - j + TQ)
    for h in range(NSA_HEADS):
        far = jnp.full((TQ, TQ), rel_ref[REL_BUCKETS - 1, h], jnp.float32)
        o_ref[h, 0] = jnp.where(i >= j, _lookup(rel_ref, b0, h), NEG_INF)
        o_ref[h, 1] = _lookup(rel_ref, b1, h)
        o_ref[h, 2] = far
        o_ref[h, 3] = jnp.where(j > i, far, NEG_INF)


def _nsa_bias_tables(rel_bias, seq):
    smem = pl.BlockSpec(memory_space=pltpu.SMEM)
    bias_c = pl.pallas_call(
        _bias_cmp_kernel,
        grid=(seq // TQ,),
        in_specs=[smem],
        out_specs=pl.BlockSpec((NSA_HEADS, TQ, N_CMP_PAD), lambda i: (0, i, 0)),
        out_shape=jax.ShapeDtypeStruct((NSA_HEADS, seq, N_CMP_PAD), jnp.float32),
        compiler_params=_cparams("parallel"),
        name="nsa_bias_cmp",
    )(rel_bias)
    bias_t = pl.pallas_call(
        _bias_tile_kernel,
        in_specs=[smem],
        out_specs=pl.BlockSpec((NSA_HEADS, 4, TQ, TQ), lambda: (0, 0, 0, 0)),
        out_shape=jax.ShapeDtypeStruct((NSA_HEADS, 4, TQ, TQ), jnp.float32),
        compiler_params=pltpu.CompilerParams(vmem_limit_bytes=VMEM_LIMIT),
        name="nsa_bias_tiles",
    )(rel_bias)
    return bias_c, bias_t


def _nsa_compress_kernel(x_ref, pe_ref, w_ref, o_ref):
    half = CMP_STRIDE * HEAD_DIM
    x = x_ref[...]
    pe = pe_ref[...]
    top = _dot(_bf16(x + pe[:, :half]), w_ref[:half, :])
    bot = _dot(_bf16(x + pe[:, half:]), w_ref[half:, :])
    n = x.shape[0]
    o_ref[...] = top + pltpu.roll(bot, n - 1, 0)


def _nsa_compress(z, col, pe, w, batch, seq):
    nchunk = seq // CMP_STRIDE
    half = CMP_STRIDE * HEAD_DIM
    x = z[:, col:col + NSA_KV_WIDTH].reshape(batch, nchunk, CMP_STRIDE, NSA_KV_HEADS, HEAD_DIM)
    x = x.transpose(0, 3, 1, 2, 4).reshape(batch, NSA_KV_HEADS, nchunk, half)
    return pl.pallas_call(
        _nsa_compress_kernel,
        grid=(batch, NSA_KV_HEADS),
        in_specs=[pl.BlockSpec((None, None, nchunk, half), lambda b, g: (b, g, 0, 0)),
                  pl.BlockSpec((1, 2 * half), lambda b, g: (0, 0)),
                  pl.BlockSpec((2 * half, HEAD_DIM), lambda b, g: (0, 0))],
        out_specs=pl.BlockSpec((None, None, nchunk, HEAD_DIM), lambda b, g: (b, g, 0, 0)),
        out_shape=jax.ShapeDtypeStruct((batch, NSA_KV_HEADS, nchunk, HEAD_DIM), jnp.float32),
        compiler_params=_cparams("parallel", "parallel"),
        name="nsa_compress",
    )(x, pe.reshape(1, 2 * half), _bf16(w.reshape(2 * half, HEAD_DIM)))


def _nsa_attn_kernel(q_ref, kc_ref, vc_ref, ks_ref, vs_ref, kw_ref, vw_ref, zg_ref, bc_ref, bt_ref,
                     ovl_ref, exp_ref, o_ref, selb_ref):
    qi = pl.program_id(2)
    r4 = NSA_REP
    q = q_ref[...]
    qb = _bf16(jnp.concatenate([q[:, r * HEAD_DIM:(r + 1) * HEAD_DIM] for r in range(r4)], axis=0))

    bias_c = bc_ref[...]
    valid = bias_c > 0.5 * NEG_INF
    s = (_dot_nt(qb, _bf16(kc_ref[...])) * ATTN_SCALE).reshape(r4, TQ, N_CMP_PAD)
    s = jnp.where(valid, s + bias_c, NEG_INF)
    e = jnp.exp(s - jnp.max(s, axis=-1, keepdims=True))
    p = jnp.where(valid, e / jnp.sum(e, axis=-1, keepdims=True), 0.0)
    o_cmp = _dot(_bf16(p.reshape(r4 * TQ, N_CMP_PAD)), _bf16(vc_ref[...])).reshape(r4, TQ, HEAD_DIM)

    psum = _bf16(p[0] + p[1] + p[2] + p[3])
    imp = _dot_nt(ovl_ref[...], psum)
    jb = lax.broadcasted_iota(jnp.int32, (N_SLC, TQ), 0)
    tq = qi * TQ + lax.broadcasted_iota(jnp.int32, (N_SLC, TQ), 1)
    cur = tq // SLC_BLOCK
    forced = ((jb == 0) | (jb == cur) | (jb == cur - 1)).astype(jnp.float32)
    score = jnp.where(jb * SLC_BLOCK <= tq, imp + FORCE_BONUS * forced, NEG_INF)
    rank = jnp.zeros((N_SLC, TQ), jnp.float32)
    for i in range(N_SLC):
        si = score[i:i + 1]
        beats = (si > score) | ((si == score) & (jb > i))
        rank = rank + beats.astype(jnp.float32)
    sel = _bf16((rank < SLC_TOPN).astype(jnp.float32))
    keymask = lax.dot_general(sel, exp_ref[...], (((0,), (0,)), ((), ())), preferred_element_type=jnp.float32)
    selb_ref[...] = (keymask - 1.0) * (-NEG_INF)

    def attend(k_ref, v_ref, n_steps, kt_of, tile_of, use_sel):
        def body(step, carry):
            m, l, acc = carry
            off = pl.multiple_of(kt_of(step) * TQ, TQ)
            kb = _bf16(k_ref[pl.ds(off, TQ), :])
            vb = _bf16(v_ref[pl.ds(off, TQ), :])
            s = (_dot_nt(qb, kb) * ATTN_SCALE).reshape(r4, TQ, TQ) + bt_ref[:, tile_of(step)]
            if use_sel:
                s = s + selb_ref[:, pl.ds(off, TQ)][None]
            m_new = jnp.maximum(m, jnp.max(s, axis=-1, keepdims=True))
            alpha = jnp.exp(m - m_new)
            pp = jnp.exp(s - m_new)
            l = alpha * l + jnp.sum(pp, axis=-1, keepdims=True)
            acc = alpha * acc + _dot(_bf16(pp.reshape(r4 * TQ, TQ)), vb).reshape(r4, TQ, HEAD_DIM)
            return m_new, l, acc

        init = (jnp.full((r4, TQ, 1), -jnp.inf, jnp.float32), jnp.zeros((r4, TQ, 1), jnp.float32),
                jnp.zeros((r4, TQ, HEAD_DIM), jnp.float32))
        _, l, acc = lax.fori_loop(0, n_steps, body, init)
        return acc / l

    o_slc = attend(ks_ref, vs_ref, qi + 1, lambda st: st, lambda st: jnp.minimum(qi - st, 2), True)
    o_win = attend(kw_ref, vw_ref, jnp.minimum(qi, WINDOW // TQ) + 1, lambda st: qi - st,
                   lambda st: jnp.where(st < 2, st, jnp.where(st < WINDOW // TQ, 2, 3)), False)

    gates = _sigmoid(zg_ref[...])
    for r in range(r4):
        o = (gates[:, 3 * r:3 * r + 1] * o_cmp[r] + gates[:, 3 * r + 1:3 * r + 2] * o_slc[r]
             + gates[:, 3 * r + 2:3 * r + 3] * o_win[r])
        o_ref[:, r * HEAD_DIM:(r + 1) * HEAD_DIM] = o


def _nsa_consts(seq):
    nc = np.arange(N_CMP_PAD)
    blk_start = nc * CMP_STRIDE
    blk_end = blk_start + CMP_BLOCK - 1
    sel_start = np.arange(N_SLC) * SLC_BLOCK
    ovl = ((blk_start[None, :] <= sel_start[:, None] + SLC_BLOCK - 1) & (blk_end[None, :] >= sel_start[:, None])
           & (nc[None, :] < N_CMP_PAD - 1))
    expand = (np.arange(seq)[None, :] // SLC_BLOCK) == np.arange(N_SLC)[:, None]
    return jnp.asarray(ovl, jnp.bfloat16), jnp.asarray(expand, jnp.bfloat16)


def _nsa_attn(z, z_gate, kc, vc, bias_c, bias_t, batch, seq):
    nq = seq // TQ
    ovl, expand = _nsa_consts(seq)
    cb = lambda c: c // HEAD_DIM
    kv = lambda col: pl.BlockSpec((seq, HEAD_DIM), lambda b, g, i: (b, cb(col) + g))
    cmp = pl.BlockSpec((None, None, N_CMP_PAD, HEAD_DIM), lambda b, g, i: (b, g, 0, 0))
    qw = NSA_REP * HEAD_DIM
    return pl.pallas_call(
        _nsa_attn_kernel,
        grid=(batch, NSA_KV_HEADS, nq),
        in_specs=[
            pl.BlockSpec((TQ, qw), lambda b, g, i: (b * nq + i, COL_QC // qw + g)),
            cmp, cmp, kv(COL_KCS), kv(COL_VCS), kv(COL_KCW), kv(COL_VCW),
            pl.BlockSpec((TQ, LANES), lambda b, g, i: (b * nq + i, 1 + g)),
            pl.BlockSpec((NSA_REP, TQ, N_CMP_PAD), lambda b, g, i: (g, i, 0)),
            pl.BlockSpec((NSA_REP, 4, TQ, TQ), lambda b, g, i: (g, 0, 0, 0)),
            pl.BlockSpec((N_SLC, N_CMP_PAD), lambda b, g, i: (0, 0)),
            pl.BlockSpec((N_SLC, seq), lambda b, g, i: (0, 0)),
        ],
        out_specs=pl.BlockSpec((TQ, qw), lambda b, g, i: (b * nq + i, g)),
        out_shape=jax.ShapeDtypeStruct((batch * seq, NSA_WIDTH), jnp.float32),
        scratch_shapes=[pltpu.VMEM((TQ, seq), jnp.float32)],
        compiler_params=_cparams("parallel", "parallel", "arbitrary"),
        name="nsa_attention",
    )(z, kc, vc, z, z, z, z, z_gate, bias_c, bias_t, ovl, expand)


def _mixnorm_kernel(oa_ref, ob_ref, oc_ref, ga_ref, gc_ref, o_ref):
    oa = oa_ref[...]
    oc = oc_ref[...]
    o_ref[:, :FOX_WIDTH] = _bf16(oa * _rms_scale(oa) * ga_ref[...])
    o_ref[:, FOX_WIDTH:FOX_WIDTH + S5_WIDTH] = ob_ref[...]
    o_ref[:, FOX_WIDTH + S5_WIDTH:] = _bf16(oc * _rms_scale(oc) * gc_ref[...])


def _mixnorm(o_a, o_b, o_c, g_a, g_c, tm=512):
    t = o_a.shape[0]
    row = lambda w: pl.BlockSpec((tm, w), lambda i: (i, 0))
    vec = lambda w: pl.BlockSpec((1, w), lambda i: (0, 0))
    return pl.pallas_call(
        _mixnorm_kernel,
        grid=(t // tm,),
        in_specs=[row(FOX_WIDTH), row(S5_WIDTH), row(NSA_WIDTH), vec(FOX_WIDTH), vec(NSA_WIDTH)],
        out_specs=row(D_MODEL),
        out_shape=jax.ShapeDtypeStruct((t, D_MODEL), jnp.bfloat16),
        compiler_params=_cparams("parallel"),
        name="mixer_norm_concat",
    )(o_a, o_b, o_c, g_a.reshape(1, -1), g_c.reshape(1, -1))


def _split_w_in(w):
    f, kv = FOX_WIDTH, NSA_KV_WIDTH
    o = 0
    qa, o = w[:, o:o + f], o + f
    ka, o = w[:, o:o + f], o + f
    va, o = w[:, o:o + f], o + f
    zf, o = w[:, o:o + FOX_HEADS], o + FOX_HEADS
    ub, o = w[:, o:o + S5_WIDTH], o + S5_WIDTH
    qc, o = w[:, o:o + NSA_WIDTH], o + NSA_WIDTH
    kvs = w[:, o:o + 6 * kv]
    o += 6 * kv
    zg = w[:, o:o + 3 * NSA_HEADS]
    main = _bf16(jnp.concatenate([qa, ka, va, ub, qc, kvs], axis=1))
    per_g = 3 * NSA_REP
    blocks = [jnp.pad(zf, ((0, 0), (0, LANES - FOX_HEADS)))]
    for g in range(NSA_KV_HEADS):
        blocks.append(jnp.pad(zg[:, g * per_g:(g + 1) * per_g], ((0, 0), (0, LANES - per_g))))
    return main, _bf16(jnp.concatenate(blocks, axis=1))


def _block_diag(blocks):
    j, n, a, b = blocks.shape
    eye = jnp.eye(n, dtype=blocks.dtype)
    return (blocks[:, :, :, None, :] * eye[None, :, None, :, None]).reshape(j, n * a, n * b)


def _s5_weights(bb_re, bb_im, c_re, c_im):
    n = S5_SLAB // S5_GROUP_CH
    inp = lambda b: _bf16(_block_diag(b.transpose(0, 2, 1).reshape(N_S5_SLABS, n, S5_GROUP_CH, S5_STATE)))
    out = lambda c: _bf16(_block_diag(c.transpose(0, 2, 1).reshape(N_S5_SLABS, n, S5_STATE, S5_GROUP_CH)))
    return inp(bb_re), inp(bb_im), out(c_re), out(c_im)


def _pad_ff(a, axis):
    pad = [(0, 0)] * a.ndim
    pad[axis] = (0, D_FF_PAD - D_FF)
    return jnp.pad(a, pad)


def kernel(x, w_in, b_forget, s5_a_re, s5_a_im, s5_log_dt, s5_b_re, s5_b_im, s5_c_re, s5_c_im, s5_d, s5_w_glu,
           cmp_pe_k, cmp_pe_v, cmp_w_k, cmp_w_v, rel_bias, g_out_fox, g_out_s5, g_out_nsa, w_out, g_pre_mix,
           g_post_mix, g_pre_ffn, g_post_ffn, w_up, conv_w, conv_b, w_down):
    batch, seq, d = x.shape
    t = batch * seq
    xf = x.reshape(t, d)

    bias_c, bias_t = _nsa_bias_tables(rel_bias, seq)
    bb_re, bb_im, pw_re, pw_im = _s5_discretize(s5_a_re, s5_a_im, s5_log_dt, s5_b_re, s5_b_im)

    hn = _rmsnorm_cast(xf, g_pre_mix[0])
    for l in range(DEPTH):
        w_main, w_gate = _split_w_in(w_in[l])
        z = _matmul(hn, w_main, 512, 1024, jnp.float32, "in_proj")
        z_gate = _matmul(hn, w_gate, 512, N_GATE, jnp.float32, "in_proj_gates")

        ccol, crow = _fox_gate(z_gate, b_forget[l], batch, seq)
        o_a = _fox_attn(z, ccol, crow, batch, seq)

        s5w = _s5_weights(bb_re[l], bb_im[l], s5_c_re[l], s5_c_im[l])
        y_b = _s5_scan(z, *s5w, s5_d[l], pw_re[l], pw_im[l], batch, seq)
        o_b = _s5_glu(y_b, _bf16(s5_w_glu[l]), g_out_s5[l])

        kc = _nsa_compress(z, COL_KCC, cmp_pe_k[l], cmp_w_k[l], batch, seq)
        vc = _nsa_compress(z, COL_VCC, cmp_pe_v[l], cmp_w_v[l], batch, seq)
        o_c = _nsa_attn(z, z_gate, kc, vc, bias_c, bias_t, batch, seq)

        mixed = _mixnorm(o_a, o_b, o_c, g_out_fox[l], g_out_nsa[l])
        xf, hn = _matmul_resid(mixed, _bf16(w_out[l]), xf, g_post_mix[l], g_pre_ffn[l], 512, 512, "out_proj")

        wg = _bf16(_pad_ff(w_up[l][:, :D_FF], 1))
        wv = _bf16(_pad_ff(w_up[l][:, D_FF:], 1))
        cwg, cwv = _pad_ff(conv_w[l][:, :D_FF], 1), _pad_ff(conv_w[l][:, D_FF:], 1)
        cbg, cbv = _pad_ff(conv_b[l][None, :D_FF], 1), _pad_ff(conv_b[l][None, D_FF:], 1)
        act = _ffn_up(hn, wg, wv, cwg, cwv, cbg, cbv, seq)
        g_next = g_pre_mix[l + 1] if l + 1 < DEPTH else jnp.ones((d,), jnp.float32)
        xf, hn = _matmul_resid(act, _bf16(_pad_ff(w_down[l], 0)), xf, g_post_ffn[l], g_next, 512, 512, "ffn_down")
    return xf.reshape(batch, seq, d)
```

```python
import functools
import math

import numpy as np
import jax
import jax.numpy as jnp
from jax import lax
from jax.experimental import pallas as pl
from jax.experimental.pallas import tpu as pltpu

D_MODEL = 4096
DEPTH = 4
HEAD_DIM = 128
FOX_WIDTH = 1024
FOX_HEADS = 8
S5_WIDTH = 1024
S5_GROUP_CH = 16
S5_GROUPS = 64
S5_STATE = 64
NSA_WIDTH = 2048
NSA_HEADS = 16
NSA_KV_HEADS = 4
NSA_REP = 4
NSA_KV_WIDTH = 512
CMP_BLOCK = 32
CMP_STRIDE = 16
SLC_BLOCK = 64
SLC_TOPN = 16
WINDOW = 512
FORCE_BONUS = 1e3
REL_BUCKETS = 32
REL_EXACT = 16
REL_MAX_DIST = 128
D_FF = 11008
RMS_EPS = 1e-6
NEG_INF = -1e30
LOG2E = 1.4426950408889634
QK_SCALE = HEAD_DIM ** -0.5 * LOG2E

LANES = 128
SUBLANES = 8
VMEM_LIMIT = 56 * 1024 * 1024

S5_SLAB = 256
S5_SLAB_STATE = S5_SLAB // S5_GROUP_CH * S5_STATE
N_S5_SLABS = S5_WIDTH // S5_SLAB
COL_QA, COL_KA, COL_VA, COL_QC, COL_KCS, COL_VCS, COL_KCW, COL_VCW = 0, 1024, 2048, 3072, 5120, 5632, 6144, 6656
N_ATTN = 7168
COL_UB, COL_KCC, COL_VCC = 0, 1024, 1536
N_F32 = 2048
N_GATE = LANES * (1 + NSA_KV_HEADS)
TQ = 128
SLC_GROUP = 4
WIN_TILES = WINDOW // TQ + 1
BIAS_TILE_FAR, BIAS_TILE_EDGE, BIAS_TILE_MASKED, N_BIAS_TILES = 2, 3, 4, 5
FOX_TQ = 512
N_CMP_PAD = 128
N_SLC = 32
ROW_TILE = 512
RESID_TK = 512
RESID_COLS = 512
RESID_ROWS = 128
FFN_TN = 512
FFN_CHUNK = 256


def _cparams(*sem):
    return pltpu.CompilerParams(dimension_semantics=sem, vmem_limit_bytes=VMEM_LIMIT)


def _bf16(x):
    return x.astype(jnp.bfloat16)


def _dot(a, b):
    return jnp.dot(a, b, preferred_element_type=jnp.float32)


def _rms_scale(x):
    return lax.rsqrt(jnp.mean(x * x, axis=-1, keepdims=True) + RMS_EPS)


def _gelu_tanh(x):
    return 0.5 * x * (1.0 + jnp.tanh(math.sqrt(2.0 / math.pi) * (x + 0.044715 * (x * x * x))))


def _sigmoid(x):
    return 1.0 / (1.0 + jnp.exp(-x))


def _rmsnorm_cast_kernel(x_ref, g_ref, o_ref):
    x = x_ref[...]
    o_ref[...] = _bf16(x * _rms_scale(x) * g_ref[...])


def _rmsnorm_cast(x, g, tm=ROW_TILE):
    t, d = x.shape
    return pl.pallas_call(
        _rmsnorm_cast_kernel,
        grid=(t // tm,),
        in_specs=[pl.BlockSpec((tm, d), lambda i: (i, 0)), pl.BlockSpec((1, d), lambda i: (0, 0))],
        out_specs=pl.BlockSpec((tm, d), lambda i: (i, 0)),
        out_shape=jax.ShapeDtypeStruct((t, d), jnp.bfloat16),
        compiler_params=_cparams("parallel"),
        name="rmsnorm_cast",
    )(x, g.reshape(1, d))


def _matmul_kernel(a_ref, w_ref, o_ref):
    o_ref[...] = _dot(a_ref[...], w_ref[...]).astype(o_ref.dtype)


def _matmul(a, w, tn, out_dtype, name, tm=ROW_TILE):
    t, k = a.shape
    n = w.shape[1]
    return pl.pallas_call(
        _matmul_kernel,
        grid=(t // tm, n // tn),
        in_specs=[pl.BlockSpec((tm, k), lambda i, j: (i, 0)), pl.BlockSpec((k, tn), lambda i, j: (0, j))],
        out_specs=pl.BlockSpec((tm, tn), lambda i, j: (i, j)),
        out_shape=jax.ShapeDtypeStruct((t, n), out_dtype),
        compiler_params=_cparams("parallel", "arbitrary"),
        name=name,
    )(a, w)


def _matmul_resid_kernel(k_total, a_ref, w_ref, x_ref, gpost_ref, gnext_ref, xo_ref, hn_ref):
    k = pl.program_id(1)
    last = pl.num_programs(1) - 1
    tm, d = xo_ref.shape
    tk = a_ref.shape[1]
    tail = k_total % tk

    @pl.when(k == 0)
    def _():
        xo_ref[...] = jnp.zeros_like(xo_ref)

    def accumulate(masked):
        a = a_ref[...]
        if masked:
            acol = lax.broadcasted_iota(jnp.int32, a.shape, 1) < tail
            a = _bf16(jnp.where(acol, a.astype(jnp.float32), 0.0))
            wrow = lax.broadcasted_iota(jnp.int32, (tk, RESID_COLS), 0) < tail
        for n in range(d // RESID_COLS):
            cols = slice(n * RESID_COLS, (n + 1) * RESID_COLS)
            w = w_ref[:, cols]
            if masked:
                w = _bf16(jnp.where(wrow, w.astype(jnp.float32), 0.0))
            xo_ref[:, cols] += _dot(a, w)

    if tail == 0:
        accumulate(False)
    else:
        pl.when(k < last)(lambda: accumulate(False))
        pl.when(k == last)(lambda: accumulate(True))

    @pl.when(k == last)
    def _():
        for r in range(tm // RESID_ROWS):
            rows = slice(r * RESID_ROWS, (r + 1) * RESID_ROWS)
            f = xo_ref[rows, :]
            xn = x_ref[rows, :] + f * _rms_scale(f) * gpost_ref[...]
            xo_ref[rows, :] = xn
            hn_ref[rows, :] = _bf16(xn * _rms_scale(xn) * gnext_ref[...])


def _matmul_resid(a, w, x, g_post, g_next, name, tm=ROW_TILE, tk=RESID_TK):
    t, kk = a.shape
    d = w.shape[1]
    return pl.pallas_call(
        functools.partial(_matmul_resid_kernel, kk),
        grid=(t // tm, pl.cdiv(kk, tk)),
        in_specs=[
            pl.BlockSpec((tm, tk), lambda i, k: (i, k)),
            pl.BlockSpec((tk, d), lambda i, k: (k, 0)),
            pl.BlockSpec((tm, d), lambda i, k: (i, 0), pipeline_mode=pl.Buffered(1)),
            pl.BlockSpec((1, d), lambda i, k: (0, 0)),
            pl.BlockSpec((1, d), lambda i, k: (0, 0)),
        ],
        out_specs=[pl.BlockSpec((tm, d), lambda i, k: (i, 0)), pl.BlockSpec((tm, d), lambda i, k: (i, 0))],
        out_shape=[jax.ShapeDtypeStruct((t, d), jnp.float32), jax.ShapeDtypeStruct((t, d), jnp.bfloat16)],
        compiler_params=_cparams("parallel", "arbitrary"),
        name=name,
    )(a, w, x, g_post.reshape(1, d), g_next.reshape(1, d))


def _ffn_up_kernel(seq, h_ref, halo_ref, wg_ref, wv_ref, cwg_ref, cwv_ref, cbg_ref, cbv_ref, o_ref):
    i = pl.program_id(0)
    tm, tn = o_ref.shape
    h = h_ref[...]
    keep = ((i * tm) % seq != 0).astype(jnp.float32)
    halo = halo_ref[...]
    row = lax.broadcasted_iota(jnp.int32, (tm, 1), 0)

    def conv(w_ref, cw_ref, cb_ref, cols):
        w = w_ref[:, cols]
        u = _dot(h, w)
        uh = _dot(halo, w) * keep
        p1 = jnp.where(row == 0, uh[7:8], pltpu.roll(u, 1, 0))
        p2 = jnp.where(row == 0, uh[6:7], jnp.where(row == 1, uh[7:8], pltpu.roll(u, 2, 0)))
        cw = cw_ref[:, cols]
        return cb_ref[:, cols] + cw[0:1] * p2 + cw[1:2] * p1 + cw[2:3] * u

    for c in range(tn // FFN_CHUNK):
        cols = slice(c * FFN_CHUNK, (c + 1) * FFN_CHUNK)
        gate = conv(wg_ref, cwg_ref, cbg_ref, cols)
        val = conv(wv_ref, cwv_ref, cbv_ref, cols)
        o_ref[:, cols] = _bf16(_gelu_tanh(gate) * val)


def _ffn_up(hn, wg, wv, cwg, cwv, cbg, cbv, seq, tm=ROW_TILE, tn=FFN_TN):
    t, d = hn.shape
    f = wg.shape[1]
    hb = tm // SUBLANES
    return pl.pallas_call(
        functools.partial(_ffn_up_kernel, seq),
        grid=(t // tm, pl.cdiv(f, tn)),
        in_specs=[
            pl.BlockSpec((tm, d), lambda i, j: (i, 0)),
            pl.BlockSpec((SUBLANES, d), lambda i, j: (jnp.maximum(i * hb - 1, 0), 0)),
            pl.BlockSpec((d, tn), lambda i, j: (0, j)),
            pl.BlockSpec((d, tn), lambda i, j: (0, j)),
            pl.BlockSpec((3, tn), lambda i, j: (0, j)),
            pl.BlockSpec((3, tn), lambda i, j: (0, j)),
            pl.BlockSpec((1, tn), lambda i, j: (0, j)),
            pl.BlockSpec((1, tn), lambda i, j: (0, j)),
        ],
        out_specs=pl.BlockSpec((tm, tn), lambda i, j: (i, j)),
        out_shape=jax.ShapeDtypeStruct((t, f), jnp.bfloat16),
        compiler_params=_cparams("parallel", "arbitrary"),
        name="ffn_up_conv_geglu",
    )(hn, hn, wg, wv, cwg, cwv, cbg, cbv)


def _softmax_step(s_chunks, m, l, keys):
    ps, alphas, ms, ls = [], [], [], []
    for c, s in enumerate(s_chunks):
        lanes = slice(c * LANES, (c + 1) * LANES)
        m_old = m[:, lanes]
        m_new = jnp.maximum(m_old, jnp.max(s, axis=0, keepdims=True))
        alpha = jnp.exp2(m_old - m_new)
        p = jnp.exp2(s - m_new)
        ls.append(alpha * l[:, lanes] + jnp.sum(p, axis=0, keepdims=True))
        ms.append(m_new)
        alphas.append(alpha)
        p = _bf16(p)
        if s.shape[0] < keys:
            p = jnp.concatenate([p, jnp.zeros((keys - s.shape[0], LANES), jnp.bfloat16)], axis=0)
        ps.append(p)
    cat = lambda xs: jnp.concatenate(xs, axis=1)
    return cat(ps), cat(alphas), cat(ms), cat(ls)


def _split3(x):
    hi = _bf16(x)
    r = x - hi.astype(jnp.float32)
    mid = _bf16(r)
    lo = _bf16(r - mid.astype(jnp.float32))
    return hi, mid, lo


def _fox_gate_kernel(zf_ref, bf_ref, ccol_ref, crow_ref):
    seq = zf_ref.shape[0]
    blk = 256
    x = zf_ref[...] + bf_ref[...]
    logf = -(jnp.maximum(-x, 0.0) + jnp.log(1.0 + jnp.exp(-jnp.abs(x))))
    r = lax.broadcasted_iota(jnp.int32, (blk, blk), 0)
    c = lax.broadcasted_iota(jnp.int32, (blk, blk), 1)
    tri = _bf16((c <= r).astype(jnp.float32))
    carry = jnp.zeros((1, LANES), jnp.float32)
    for b in range(seq // blk):
        hi, mid, lo = _split3(logf[b * blk:(b + 1) * blk])
        cs = _dot(tri, hi) + _dot(tri, mid) + _dot(tri, lo) + carry
        ccol_ref[b * blk:(b + 1) * blk, :] = cs
        carry = cs[blk - 1:blk]
    crow_ref[...] = ccol_ref[...].T


def _fox_gate(z_gate, b_f, batch, seq):
    bf = jnp.zeros((1, LANES), jnp.float32).at[0, :FOX_HEADS].set(b_f)
    return pl.pallas_call(
        _fox_gate_kernel,
        grid=(batch,),
        in_specs=[pl.BlockSpec((seq, LANES), lambda b: (b, 0)), pl.BlockSpec((1, LANES), lambda b: (0, 0))],
        out_specs=[pl.BlockSpec((seq, LANES), lambda b: (b, 0)), pl.BlockSpec((None, LANES, seq), lambda b: (b, 0, 0))],
        out_shape=[jax.ShapeDtypeStruct((batch * seq, LANES), jnp.float32),
                   jax.ShapeDtypeStruct((batch, LANES, seq), jnp.float32)],
        compiler_params=_cparams("parallel"),
        name="fox_gate_cumsum",
    )(z_gate, bf)


def _fox_attn_kernel(q_ref, k_ref, vt_ref, cq_ref, ck_ref, o_ref, acc_ref):
    qi = pl.program_id(2)
    tq = q_ref.shape[0]
    nch = tq // LANES
    qt = _bf16(q_ref[...].astype(jnp.float32).T)
    cq = cq_ref[...] * LOG2E
    acc_ref[...] = jnp.zeros_like(acc_ref)

    def step(kt, m, l, diagonal):
        off = pl.multiple_of(kt * tq, tq)
        st = _dot(k_ref[pl.ds(off, tq), :], qt)
        ck = ck_ref[pl.ds(off, tq), :] * LOG2E
        chunks = []
        for c in range(nch):
            lanes = slice(c * LANES, (c + 1) * LANES)
            if diagonal:
                rows = (c + 1) * LANES
                s = st[:rows, lanes] * QK_SCALE + (cq[:, lanes] - ck[:rows])
                krow = lax.broadcasted_iota(jnp.int32, (rows, LANES), 0)
                qcol = c * LANES + lax.broadcasted_iota(jnp.int32, (rows, LANES), 1)
                s = jnp.where(krow <= qcol, s, NEG_INF)
            else:
                s = st[:, lanes] * QK_SCALE + (cq[:, lanes] - ck)
            chunks.append(s)
        p, alpha, m, l = _softmax_step(chunks, m, l, tq)
        acc_ref[...] = acc_ref[...] * alpha + _dot(vt_ref[:, pl.ds(off, tq)], p)
        return m, l

    init = (jnp.full((1, tq), -jnp.inf, jnp.float32), jnp.zeros((1, tq), jnp.float32))
    m, l = lax.fori_loop(0, qi, lambda kt, c: step(kt, c[0], c[1], False), init)
    _, l = step(qi, m, l, True)
    o_ref[...] = (acc_ref[...] * (1.0 / l)).T


def _heads_t(z, col, heads, batch, seq):
    return z[:, col:col + heads * HEAD_DIM].reshape(batch, seq, heads, HEAD_DIM).transpose(0, 2, 3, 1)


def _fox_attn(za, ccol, crow, batch, seq, tq=FOX_TQ):
    nq = seq // tq
    ck = ccol[:, :FOX_HEADS].reshape(batch, seq, FOX_HEADS).transpose(0, 2, 1).reshape(batch, FOX_HEADS, seq, 1)
    cq = crow[:, :FOX_HEADS, :].reshape(batch, FOX_HEADS, 1, seq)
    vt = _heads_t(za, COL_VA, FOX_HEADS, batch, seq)
    cb = lambda c: c // HEAD_DIM
    return pl.pallas_call(
        _fox_attn_kernel,
        grid=(batch, FOX_HEADS, nq),
        in_specs=[
            pl.BlockSpec((tq, HEAD_DIM), lambda b, h, i: (b * nq + i, cb(COL_QA) + h)),
            pl.BlockSpec((seq, HEAD_DIM), lambda b, h, i: (b, cb(COL_KA) + h)),
            pl.BlockSpec((None, None, HEAD_DIM, seq), lambda b, h, i: (b, h, 0, 0)),
            pl.BlockSpec((None, None, 1, tq), lambda b, h, i: (b, h, 0, i)),
            pl.BlockSpec((None, None, seq, 1), lambda b, h, i: (b, h, 0, 0)),
        ],
        out_specs=pl.BlockSpec((tq, HEAD_DIM), lambda b, h, i: (b * nq + i, h)),
        out_shape=jax.ShapeDtypeStruct((batch * seq, FOX_WIDTH), jnp.float32),
        scratch_shapes=[pltpu.VMEM((HEAD_DIM, tq), jnp.float32)],
        compiler_params=_cparams("parallel", "parallel", "arbitrary"),
        name="fox_attention",
    )(za, za, vt, cq, ck)


def _s5_disc_kernel(are_ref, aim_ref, ldt_ref, bre_ref, bim_ref, bbre_ref, bbim_ref, pwre_ref, pwim_ref):
    lam_re = jnp.minimum(are_ref[...], -1e-4)
    lam_im = aim_ref[...]
    dt = jnp.exp(ldt_ref[...])
    mag = jnp.exp(lam_re * dt)
    ang = lam_im * dt
    lb_re, lb_im = mag * jnp.cos(ang), mag * jnp.sin(ang)
    den = lam_re * lam_re + lam_im * lam_im
    nr, ni = lb_re - 1.0, lb_im
    coef_re = (nr * lam_re + ni * lam_im) / den
    coef_im = (ni * lam_re - nr * lam_im) / den
    b_re, b_im = bre_ref[...], bim_ref[...]
    bbre_ref[...] = coef_re * b_re - coef_im * b_im
    bbim_ref[...] = coef_re * b_im + coef_im * b_re
    p_re, p_im = lb_re, lb_im
    for i in range(SUBLANES):
        pwre_ref[i:i + 1, :] = p_re
        pwim_ref[i:i + 1, :] = p_im
        p_re, p_im = p_re * lb_re - p_im * lb_im, p_re * lb_im + p_im * lb_re


def _s5_discretize(a_re, a_im, log_dt, b_re, b_im):
    l, g, p = a_re.shape
    h = b_re.shape[-1]
    n = l * g * p
    row = lambda a: a.reshape(1, n)
    ldt = jnp.broadcast_to(log_dt[:, :, None], (l, g, p))
    bt = lambda b: b.reshape(n, h).T
    full = lambda r: pl.BlockSpec((r, n), lambda: (0, 0))
    bbre, bbim, pwre, pwim = pl.pallas_call(
        _s5_disc_kernel,
        in_specs=[full(1), full(1), full(1), full(h), full(h)],
        out_specs=[full(h), full(h), full(SUBLANES), full(SUBLANES)],
        out_shape=[jax.ShapeDtypeStruct((h, n), jnp.float32)] * 2 + [jax.ShapeDtypeStruct((SUBLANES, n), jnp.float32)] * 2,
        compiler_params=pltpu.CompilerParams(vmem_limit_bytes=VMEM_LIMIT),
        name="s5_discretize",
    )(row(a_re), row(a_im), row(ldt), bt(b_re), bt(b_im))
    unb = lambda b: b.T.reshape(l, g, p, h)
    unp = lambda q: q.reshape(SUBLANES, l, g * p).transpose(1, 0, 2)
    return unb(bbre), unb(bbim), unp(pwre), unp(pwim)


def _s5_scan_kernel(u_ref, bre_ref, bim_ref, cre_ref, cim_ref, d_ref, pwre_ref, pwim_ref, y_ref, hre_ref, him_ref):
    seq = u_ref.shape[0]
    u = u_ref[...]
    ub = _bf16(u)
    hre_ref[...] = _dot(ub, bre_ref[...])
    him_ref[...] = _dot(ub, bim_ref[...])

    pw_re, pw_im = pwre_ref[...], pwim_ref[...]
    sub = lax.broadcasted_iota(jnp.int32, pw_re.shape, 0)

    def shifted(i, k):
        m = sub >= k
        return (jnp.where(m, pw_re[i:i + 1], 0.0), jnp.where(m, pw_im[i:i + 1], 0.0))

    steps = [(1,) + shifted(0, 1), (2,) + shifted(1, 2), (4,) + shifted(3, 4)]

    def tile(j, carry):
        c_re, c_im = carry
        off = pl.multiple_of(j * SUBLANES, SUBLANES)
        x_re = hre_ref[pl.ds(off, SUBLANES), :]
        x_im = him_ref[pl.ds(off, SUBLANES), :]
        for k, a_re, a_im in steps:
            r_re, r_im = pltpu.roll(x_re, k, 0), pltpu.roll(x_im, k, 0)
            x_re, x_im = (x_re + a_re * r_re - a_im * r_im, x_im + a_re * r_im + a_im * r_re)
        x_re, x_im = (x_re + pw_re * c_re - pw_im * c_im, x_im + pw_re * c_im + pw_im * c_re)
        hre_ref[pl.ds(off, SUBLANES), :] = x_re
        him_ref[pl.ds(off, SUBLANES), :] = x_im
        return x_re[SUBLANES - 1:SUBLANES], x_im[SUBLANES - 1:SUBLANES]

    zero = jnp.zeros((1, pw_re.shape[1]), jnp.float32)
    lax.fori_loop(0, seq // SUBLANES, tile, (zero, zero))

    y = _dot(_bf16(hre_ref[...]), cre_ref[...]) - _dot(_bf16(him_ref[...]), cim_ref[...]) + d_ref[...] * u
    y_ref[...] = _gelu_tanh(y)


def _s5_scan(zb, b_big_re, b_big_im, c_big_re, c_big_im, d_skip, pw_re, pw_im, batch, seq):
    w = S5_SLAB_STATE
    return pl.pallas_call(
        _s5_scan_kernel,
        grid=(batch, N_S5_SLABS),
        in_specs=[
            pl.BlockSpec((seq, S5_SLAB), lambda b, j: (b, COL_UB // S5_SLAB + j)),
            pl.BlockSpec((None, S5_SLAB, w), lambda b, j: (j, 0, 0)),
            pl.BlockSpec((None, S5_SLAB, w), lambda b, j: (j, 0, 0)),
            pl.BlockSpec((None, w, S5_SLAB), lambda b, j: (j, 0, 0)),
            pl.BlockSpec((None, w, S5_SLAB), lambda b, j: (j, 0, 0)),
            pl.BlockSpec((1, S5_SLAB), lambda b, j: (0, j)),
            pl.BlockSpec((SUBLANES, w), lambda b, j: (0, j)),
            pl.BlockSpec((SUBLANES, w), lambda b, j: (0, j)),
        ],
        out_specs=pl.BlockSpec((seq, S5_SLAB), lambda b, j: (b, j)),
        out_shape=jax.ShapeDtypeStruct((batch * seq, S5_WIDTH), jnp.float32),
        scratch_shapes=[pltpu.VMEM((seq, w), jnp.float32), pltpu.VMEM((seq, w), jnp.float32)],
        compiler_params=_cparams("parallel", "arbitrary"),
        name="s5_scan",
    )(zb, b_big_re, b_big_im, c_big_re, c_big_im, d_skip.reshape(1, S5_WIDTH), pw_re, pw_im)


def _s5_glu_kernel(y_ref, w_ref, g_ref, o_ref):
    y = y_ref[...]
    o = y * _sigmoid(_dot(_bf16(y), w_ref[...]))
    o_ref[...] = _bf16(o * _rms_scale(o) * g_ref[...])


def _s5_glu(y, w_glu, g_out, tm=ROW_TILE):
    t, d = y.shape
    return pl.pallas_call(
        _s5_glu_kernel,
        grid=(t // tm,),
        in_specs=[pl.BlockSpec((tm, d), lambda i: (i, 0)), pl.BlockSpec((d, d), lambda i: (0, 0)),
                  pl.BlockSpec((1, d), lambda i: (0, 0))],
        out_specs=pl.BlockSpec((tm, d), lambda i: (i, 0)),
        out_shape=jax.ShapeDtypeStruct((t, d), jnp.bfloat16),
        compiler_params=_cparams("parallel"),
        name="s5_glu_norm",
    )(y, w_glu, g_out.reshape(1, d))


def _t5_bucket(dist):
    n = jnp.maximum(dist, 0)
    nf = jnp.maximum(n, 1).astype(jnp.float32)
    large = REL_EXACT + (jnp.log(nf / REL_EXACT) / math.log(REL_MAX_DIST / REL_EXACT)
                         * (REL_BUCKETS - REL_EXACT)).astype(jnp.int32)
    return jnp.where(n < REL_EXACT, n, jnp.minimum(large, REL_BUCKETS - 1))


def _lookup(rel_ref, bucket, h):
    out = jnp.zeros(bucket.shape, jnp.float32)
    for b in range(REL_BUCKETS):
        out = jnp.where(bucket == b, rel_ref[b, h] * LOG2E, out)
    return out


def _bias_cmp_kernel(rel_ref, o_ref):
    qi = pl.program_id(0)
    n = lax.broadcasted_iota(jnp.int32, (N_CMP_PAD, TQ), 0)
    t = qi * TQ + lax.broadcasted_iota(jnp.int32, (N_CMP_PAD, TQ), 1)
    dist = t - (n * CMP_STRIDE + CMP_BLOCK - 1)
    valid = (dist >= 0) & (n < N_CMP_PAD - 1)
    bucket = _t5_bucket(dist)
    for h in range(NSA_HEADS):
        o_ref[h] = jnp.where(valid, _lookup(rel_ref, bucket, h), NEG_INF)


def _bias_tile_kernel(rel_ref, o_ref):
    j = lax.broadcasted_iota(jnp.int32, (TQ, TQ), 0)
    i = lax.broadcasted_iota(jnp.int32, (TQ, TQ), 1)
    b0 = _t5_bucket(i - j)
    b1 = _t5_bucket(i - j + TQ)
    for h in range(NSA_HEADS):
        far = jnp.full((TQ, TQ), rel_ref[REL_BUCKETS - 1, h] * LOG2E, jnp.float32)
        o_ref[h, 0] = jnp.where(i >= j, _lookup(rel_ref, b0, h), NEG_INF)
        o_ref[h, 1] = _lookup(rel_ref, b1, h)
        o_ref[h, BIAS_TILE_FAR] = far
        o_ref[h, BIAS_TILE_EDGE] = jnp.where(j > i, far, NEG_INF)
        o_ref[h, BIAS_TILE_MASKED] = jnp.full((TQ, TQ), NEG_INF, jnp.float32)


def _nsa_bias_tables(rel_bias, seq):
    smem = pl.BlockSpec(memory_space=pltpu.SMEM)
    nq = seq // TQ
    bias_c = pl.pallas_call(
        _bias_cmp_kernel,
        grid=(nq,),
        in_specs=[smem],
        out_specs=pl.BlockSpec((NSA_HEADS, None, N_CMP_PAD, TQ), lambda i: (0, i, 0, 0)),
        out_shape=jax.ShapeDtypeStruct((NSA_HEADS, nq, N_CMP_PAD, TQ), jnp.float32),
        compiler_params=_cparams("parallel"),
        name="nsa_bias_cmp",
    )(rel_bias)
    bias_t = pl.pallas_call(
        _bias_tile_kernel,
        in_specs=[smem],
        out_specs=pl.BlockSpec((NSA_HEADS, N_BIAS_TILES, TQ, TQ), lambda: (0, 0, 0, 0)),
        out_shape=jax.ShapeDtypeStruct((NSA_HEADS, N_BIAS_TILES, TQ, TQ), jnp.float32),
        compiler_params=pltpu.CompilerParams(vmem_limit_bytes=VMEM_LIMIT),
        name="nsa_bias_tiles",
    )(rel_bias)
    return bias_c, bias_t


def _nsa_compress_kernel(transpose_out, x_ref, pe_ref, w_ref, o_ref):
    half = CMP_STRIDE * HEAD_DIM
    x = x_ref[...]
    pe = pe_ref[...]
    top = _dot(_bf16(x + pe[:, :half]), w_ref[:half, :])
    bot = _dot(_bf16(x + pe[:, half:]), w_ref[half:, :])
    n = x.shape[0]
    out = top + pltpu.roll(bot, n - 1, 0)
    o_ref[...] = _bf16(out.T if transpose_out else out)


def _nsa_compress(zb, col, pe, w, batch, seq, transpose_out):
    nchunk = seq // CMP_STRIDE
    half = CMP_STRIDE * HEAD_DIM
    x = zb[:, col:col + NSA_KV_WIDTH].reshape(batch, nchunk, CMP_STRIDE, NSA_KV_HEADS, HEAD_DIM)
    x = x.transpose(0, 3, 1, 2, 4).reshape(batch, NSA_KV_HEADS, nchunk, half)
    return pl.pallas_call(
        functools.partial(_nsa_compress_kernel, transpose_out),
        grid=(batch, NSA_KV_HEADS),
        in_specs=[pl.BlockSpec((None, None, nchunk, half), lambda b, g: (b, g, 0, 0)),
                  pl.BlockSpec((1, 2 * half), lambda b, g: (0, 0)),
                  pl.BlockSpec((2 * half, HEAD_DIM), lambda b, g: (0, 0))],
        out_specs=pl.BlockSpec((None, None, nchunk, HEAD_DIM), lambda b, g: (b, g, 0, 0)),
        out_shape=jax.ShapeDtypeStruct((batch, NSA_KV_HEADS, nchunk, HEAD_DIM), jnp.bfloat16),
        compiler_params=_cparams("parallel", "parallel"),
        name="nsa_compress",
    )(x, pe.reshape(1, 2 * half), _bf16(w.reshape(2 * half, HEAD_DIM)))


def _nsa_attn_kernel(q_ref, kc_ref, vct_ref, ks_ref, vst_ref, kw_ref, vwt_ref, zg_ref, bc_ref, bt_ref,
                     ovl_ref, exp_ref, o_ref, qt_ref, acc_ref, ocmp_ref, oslc_ref):
    qi = pl.program_id(2)
    r4 = NSA_REP
    q = q_ref[...].astype(jnp.float32)
    for r in range(r4):
        qt_ref[:, r * TQ:(r + 1) * TQ] = _bf16(q[:, r * HEAD_DIM:(r + 1) * HEAD_DIM].T)
    qt = qt_ref[...]

    st = _dot(kc_ref[...], qt)
    psum = jnp.zeros((N_CMP_PAD, TQ), jnp.float32)
    ps = []
    for r in range(r4):
        bias = bc_ref[r]
        valid = bias > 0.5 * NEG_INF
        s = jnp.where(valid, st[:, r * TQ:(r + 1) * TQ] * QK_SCALE + bias, NEG_INF)
        e = jnp.exp2(s - jnp.max(s, axis=0, keepdims=True))
        p = jnp.where(valid, e * (1.0 / jnp.sum(e, axis=0, keepdims=True)), 0.0)
        psum = psum + p
        ps.append(_bf16(p))
    ocmp_ref[...] = _dot(vct_ref[...], jnp.concatenate(ps, axis=1))

    imp = _dot(ovl_ref[...], _bf16(psum))
    jb = lax.broadcasted_iota(jnp.int32, (N_SLC, TQ), 0)
    tq = qi * TQ + lax.broadcasted_iota(jnp.int32, (N_SLC, TQ), 1)
    cur = tq // SLC_BLOCK
    forced = ((jb == 0) | (jb == cur) | (jb == cur - 1)).astype(jnp.float32)
    score = jnp.where(jb * SLC_BLOCK <= tq, imp + FORCE_BONUS * forced, NEG_INF)
    rank = jnp.zeros((N_SLC, TQ), jnp.float32)
    for i in range(N_SLC):
        si = score[i:i + 1]
        beats = (si > score) | ((si == score) & (jb > i))
        rank = rank + beats.astype(jnp.float32)
    sel = _bf16((rank < SLC_TOPN).astype(jnp.float32))

    def logits(k_ref, first_tile, n_tiles, tile_of_dist):
        off = pl.multiple_of(first_tile * TQ, TQ)
        st = _dot(k_ref[pl.ds(off, n_tiles * TQ), :], qt)
        tiles = [tile_of_dist(qi - (first_tile + u)) for u in range(n_tiles)]
        out = []
        for r in range(r4):
            bias = jnp.concatenate([bt_ref[r, tl] for tl in tiles], axis=0)
            out.append(st[:, r * TQ:(r + 1) * TQ] * QK_SCALE + bias)
        return off, out

    slc_tile = lambda d: jnp.where(d < 0, BIAS_TILE_MASKED, jnp.minimum(d, BIAS_TILE_FAR))
    keys = SLC_GROUP * TQ
    acc_ref[...] = jnp.zeros_like(acc_ref)

    def slc_body(it, carry):
        m, l = carry
        off, chunks = logits(ks_ref, it * SLC_GROUP, SLC_GROUP, slc_tile)
        selb = (_dot(exp_ref[pl.ds(off, keys), :], sel) - 1.0) * (-NEG_INF)
        p, alpha, m, l = _softmax_step([s + selb for s in chunks], m, l, keys)
        acc_ref[...] = acc_ref[...] * alpha + _dot(vst_ref[:, pl.ds(off, keys)], p)
        return m, l

    init = (jnp.full((1, r4 * TQ), -jnp.inf, jnp.float32), jnp.zeros((1, r4 * TQ), jnp.float32))
    _, l = lax.fori_loop(0, qi // SLC_GROUP + 1, slc_body, init)
    oslc_ref[...] = acc_ref[...] * (1.0 / l)

    far_d = WINDOW // TQ
    win_tile = lambda d: jnp.where(d < 0, BIAS_TILE_MASKED, jnp.where(
        d < BIAS_TILE_FAR, d, jnp.where(d < far_d, BIAS_TILE_FAR, BIAS_TILE_EDGE)))
    off, chunks = logits(kw_ref, jnp.maximum(qi - far_d, 0), WIN_TILES, win_tile)
    p, _, _, l = _softmax_step(chunks, init[0], init[1], WIN_TILES * TQ)
    acc_ref[...] = _dot(vwt_ref[:, pl.ds(off, WIN_TILES * TQ)], p) * (1.0 / l)

    gt = _sigmoid(zg_ref[...]).T
    for r in range(r4):
        lanes = slice(r * TQ, (r + 1) * TQ)
        o = (gt[3 * r:3 * r + 1] * ocmp_ref[:, lanes] + gt[3 * r + 1:3 * r + 2] * oslc_ref[:, lanes]
             + gt[3 * r + 2:3 * r + 3] * acc_ref[:, lanes])
        o_ref[:, r * HEAD_DIM:(r + 1) * HEAD_DIM] = o.T


def _nsa_consts(seq):
    nc = np.arange(N_CMP_PAD)
    blk_start = nc * CMP_STRIDE
    blk_end = blk_start + CMP_BLOCK - 1
    sel_start = np.arange(N_SLC) * SLC_BLOCK
    ovl = ((blk_start[None, :] <= sel_start[:, None] + SLC_BLOCK - 1) & (blk_end[None, :] >= sel_start[:, None])
           & (nc[None, :] < N_CMP_PAD - 1))
    expand = (np.arange(seq)[:, None] // SLC_BLOCK) == np.arange(N_SLC)[None, :]
    return jnp.asarray(ovl, jnp.bfloat16), jnp.asarray(expand, jnp.bfloat16)


def _nsa_attn(za, z_gate, kc, vct, bias_c, bias_t, batch, seq):
    nq = seq // TQ
    ovl, expand = _nsa_consts(seq)
    vst = _heads_t(za, COL_VCS, NSA_KV_HEADS, batch, seq)
    vwt = _heads_t(za, COL_VCW, NSA_KV_HEADS, batch, seq)
    cb = lambda c: c // HEAD_DIM
    keys = lambda col: pl.BlockSpec((seq, HEAD_DIM), lambda b, g, i: (b, cb(col) + g))
    vals = pl.BlockSpec((None, None, HEAD_DIM, seq), lambda b, g, i: (b, g, 0, 0))
    cmp = pl.BlockSpec((None, None, N_CMP_PAD, HEAD_DIM), lambda b, g, i: (b, g, 0, 0))
    qw = NSA_REP * HEAD_DIM
    wide = pltpu.VMEM((HEAD_DIM, NSA_REP * TQ), jnp.float32)
    return pl.pallas_call(
        _nsa_attn_kernel,
        grid=(batch, NSA_KV_HEADS, nq),
        in_specs=[
            pl.BlockSpec((TQ, qw), lambda b, g, i: (b * nq + i, COL_QC // qw + g)),
            cmp, cmp, keys(COL_KCS), vals, keys(COL_KCW), vals,
            pl.BlockSpec((TQ, LANES), lambda b, g, i: (b * nq + i, 1 + g)),
            pl.BlockSpec((NSA_REP, None, N_CMP_PAD, TQ), lambda b, g, i: (g, i, 0, 0)),
            pl.BlockSpec((NSA_REP, N_BIAS_TILES, TQ, TQ), lambda b, g, i: (g, 0, 0, 0)),
            pl.BlockSpec((N_SLC, N_CMP_PAD), lambda b, g, i: (0, 0)),
            pl.BlockSpec((seq, N_SLC), lambda b, g, i: (0, 0)),
        ],
        out_specs=pl.BlockSpec((TQ, qw), lambda b, g, i: (b * nq + i, g)),
        out_shape=jax.ShapeDtypeStruct((batch * seq, NSA_WIDTH), jnp.float32),
        scratch_shapes=[pltpu.VMEM((HEAD_DIM, NSA_REP * TQ), jnp.bfloat16), wide, wide, wide],
        compiler_params=_cparams("parallel", "parallel", "arbitrary"),
        name="nsa_attention",
    )(za, kc, vct, za, vst, za, vwt, z_gate, bias_c, bias_t, ovl, expand)


def _mixnorm_kernel(oa_ref, ob_ref, oc_ref, ga_ref, gc_ref, o_ref):
    oa = oa_ref[...]
    oc = oc_ref[...]
    o_ref[:, :FOX_WIDTH] = _bf16(oa * _rms_scale(oa) * ga_ref[...])
    o_ref[:, FOX_WIDTH:FOX_WIDTH + S5_WIDTH] = ob_ref[...]
    o_ref[:, FOX_WIDTH + S5_WIDTH:] = _bf16(oc * _rms_scale(oc) * gc_ref[...])


def _mixnorm(o_a, o_b, o_c, g_a, g_c, tm=ROW_TILE):
    t = o_a.shape[0]
    row = lambda w: pl.BlockSpec((tm, w), lambda i: (i, 0))
    vec = lambda w: pl.BlockSpec((1, w), lambda i: (0, 0))
    return pl.pallas_call(
        _mixnorm_kernel,
        grid=(t // tm,),
        in_specs=[row(FOX_WIDTH), row(S5_WIDTH), row(NSA_WIDTH), vec(FOX_WIDTH), vec(NSA_WIDTH)],
        out_specs=row(D_MODEL),
        out_shape=jax.ShapeDtypeStruct((t, D_MODEL), jnp.bfloat16),
        compiler_params=_cparams("parallel"),
        name="mixer_norm_concat",
    )(o_a, o_b, o_c, g_a.reshape(1, -1), g_c.reshape(1, -1))


def _split_w_in(w):
    f, kv = FOX_WIDTH, NSA_KV_WIDTH
    o = 0
    qa, o = w[:, o:o + f], o + f
    ka, o = w[:, o:o + f], o + f
    va, o = w[:, o:o + f], o + f
    zf, o = w[:, o:o + FOX_HEADS], o + FOX_HEADS
    ub, o = w[:, o:o + S5_WIDTH], o + S5_WIDTH
    qc, o = w[:, o:o + NSA_WIDTH], o + NSA_WIDTH
    kcc_vcc, o = w[:, o:o + 2 * kv], o + 2 * kv
    rest, o = w[:, o:o + 4 * kv], o + 4 * kv
    zg = w[:, o:o + 3 * NSA_HEADS]
    w_attn = _bf16(jnp.concatenate([qa, ka, va, qc, rest], axis=1))
    w_f32 = _bf16(jnp.concatenate([ub, kcc_vcc], axis=1))
    per_g = 3 * NSA_REP
    blocks = [jnp.pad(zf, ((0, 0), (0, LANES - FOX_HEADS)))]
    for g in range(NSA_KV_HEADS):
        blocks.append(jnp.pad(zg[:, g * per_g:(g + 1) * per_g], ((0, 0), (0, LANES - per_g))))
    return w_attn, w_f32, _bf16(jnp.concatenate(blocks, axis=1))


def _block_diag(blocks):
    j, n, a, b = blocks.shape
    eye = jnp.eye(n, dtype=blocks.dtype)
    return (blocks[:, :, :, None, :] * eye[None, :, None, :, None]).reshape(j, n * a, n * b)


def _s5_weights(bb_re, bb_im, c_re, c_im):
    n = S5_SLAB // S5_GROUP_CH
    inp = lambda b: _bf16(_block_diag(b.transpose(0, 2, 1).reshape(N_S5_SLABS, n, S5_GROUP_CH, S5_STATE)))
    out = lambda c: _bf16(_block_diag(c.transpose(0, 2, 1).reshape(N_S5_SLABS, n, S5_STATE, S5_GROUP_CH)))
    return inp(bb_re), inp(bb_im), out(c_re), out(c_im)


def kernel(x, w_in, b_forget, s5_a_re, s5_a_im, s5_log_dt, s5_b_re, s5_b_im, s5_c_re, s5_c_im, s5_d, s5_w_glu,
           cmp_pe_k, cmp_pe_v, cmp_w_k, cmp_w_v, rel_bias, g_out_fox, g_out_s5, g_out_nsa, w_out, g_pre_mix,
           g_post_mix, g_pre_ffn, g_post_ffn, w_up, conv_w, conv_b, w_down):
    batch, seq, d = x.shape
    t = batch * seq
    xf = x.reshape(t, d)

    bias_c, bias_t = _nsa_bias_tables(rel_bias, seq)
    bb_re, bb_im, pw_re, pw_im = _s5_discretize(s5_a_re, s5_a_im, s5_log_dt, s5_b_re, s5_b_im)

    hn = _rmsnorm_cast(xf, g_pre_mix[0])
    for l in range(DEPTH):
        w_attn, w_f32, w_gate = _split_w_in(w_in[l])
        za = _matmul(hn, w_attn, 1024, jnp.bfloat16, "in_proj_attn")
        zb = _matmul(hn, w_f32, 1024, jnp.float32, "in_proj_f32")
        z_gate = _matmul(hn, w_gate, N_GATE, jnp.float32, "in_proj_gates")

        ccol, crow = _fox_gate(z_gate, b_forget[l], batch, seq)
        o_a = _fox_attn(za, ccol, crow, batch, seq)

        s5w = _s5_weights(bb_re[l], bb_im[l], s5_c_re[l], s5_c_im[l])
        y_b = _s5_scan(zb, *s5w, s5_d[l], pw_re[l], pw_im[l], batch, seq)
        o_b = _s5_glu(y_b, _bf16(s5_w_glu[l]), g_out_s5[l])

        kc = _nsa_compress(zb, COL_KCC, cmp_pe_k[l], cmp_w_k[l], batch, seq, False)
        vct = _nsa_compress(zb, COL_VCC, cmp_pe_v[l], cmp_w_v[l], batch, seq, True)
        o_c = _nsa_attn(za, z_gate, kc, vct, bias_c, bias_t, batch, seq)

        mixed = _mixnorm(o_a, o_b, o_c, g_out_fox[l], g_out_nsa[l])
        xf, hn = _matmul_resid(mixed, _bf16(w_out[l]), xf, g_post_mix[l], g_pre_ffn[l], "out_proj")

        wg, wv = _bf16(w_up[l][:, :D_FF]), _bf16(w_up[l][:, D_FF:])
        cwg, cwv = conv_w[l][:, :D_FF], conv_w[l][:, D_FF:]
        cbg, cbv = conv_b[l][None, :D_FF], conv_b[l][None, D_FF:]
        act = _ffn_up(hn, wg, wv, cwg, cwv, cbg, cbv, seq)
        g_next = g_pre_mix[l + 1] if l + 1 < DEPTH else jnp.ones((d,), jnp.float32)
        xf, hn = _matmul_resid(act, _bf16(w_down[l]), xf, g_post_ffn[l], g_next, "ffn_down")
    return xf.reshape(batch, seq, d)
```

```python
import functools
import math

import numpy as np
import jax
import jax.numpy as jnp
from jax import lax
from jax.experimental import pallas as pl
from jax.experimental.pallas import tpu as pltpu

D_MODEL = 4096
DEPTH = 4
HEAD_DIM = 128
FOX_WIDTH = 1024
FOX_HEADS = 8
S5_WIDTH = 1024
S5_GROUP_CH = 16
S5_GROUPS = 64
S5_STATE = 64
NSA_WIDTH = 2048
NSA_HEADS = 16
NSA_KV_HEADS = 4
NSA_REP = 4
NSA_KV_WIDTH = 512
CMP_BLOCK = 32
CMP_STRIDE = 16
SLC_BLOCK = 64
SLC_TOPN = 16
WINDOW = 512
FORCE_BONUS = 1e3
REL_BUCKETS = 32
REL_EXACT = 16
REL_MAX_DIST = 128
D_FF = 11008
RMS_EPS = 1e-6
NEG_INF = -1e30
LOG2E = 1.4426950408889634
QK_SCALE = HEAD_DIM ** -0.5 * LOG2E

LANES = 128
SUBLANES = 8
VMEM_LIMIT = 56 * 1024 * 1024

S5_SLAB = 256
S5_SLAB_STATE = S5_SLAB // S5_GROUP_CH * S5_STATE
N_S5_SLABS = S5_WIDTH // S5_SLAB
COL_QA, COL_KA, COL_VA, COL_QC, COL_KCS, COL_VCS, COL_KCW, COL_VCW = 0, 1024, 2048, 3072, 5120, 5632, 6144, 6656
N_ATTN = 7168
COL_UB, COL_KCC, COL_VCC, COL_GATE = 0, 1024, 1536, 2048
N_GATE = LANES * (1 + NSA_KV_HEADS)
N_F32 = COL_GATE + N_GATE
TQ = 128
SLC_GROUP = 4
WIN_TILES = WINDOW // TQ + 1
BIAS_TILE_FAR, BIAS_TILE_EDGE, BIAS_TILE_MASKED, N_BIAS_TILES = 2, 3, 4, 5
FOX_TQ = 512
N_CMP_PAD = 128
N_SLC = 32
ROW_TILE = 512
RESID_TK = 512
RESID_COLS = 512
RESID_ROWS = 128
FFN_TN = 512
FFN_CHUNK = 256
FFN_ROWS = 128
FFN_HALO = 16


def _cparams(*sem):
    return pltpu.CompilerParams(dimension_semantics=sem, vmem_limit_bytes=VMEM_LIMIT)


def _bf16(x):
    return x.astype(jnp.bfloat16)


def _dot(a, b):
    return jnp.dot(a, b, preferred_element_type=jnp.float32)


def _rms_scale(x):
    return lax.rsqrt(jnp.mean(x * x, axis=-1, keepdims=True) + RMS_EPS)


def _gelu_tanh(x):
    return 0.5 * x * (1.0 + jnp.tanh(math.sqrt(2.0 / math.pi) * (x + 0.044715 * (x * x * x))))


def _sigmoid(x):
    return 1.0 / (1.0 + jnp.exp(-x))


def _rmsnorm_cast_kernel(x_ref, g_ref, o_ref):
    x = x_ref[...]
    o_ref[...] = _bf16(x * _rms_scale(x) * g_ref[...])


def _rmsnorm_cast(x, g, tm=ROW_TILE):
    t, d = x.shape
    return pl.pallas_call(
        _rmsnorm_cast_kernel,
        grid=(t // tm,),
        in_specs=[pl.BlockSpec((tm, d), lambda i: (i, 0)), pl.BlockSpec((1, d), lambda i: (0, 0))],
        out_specs=pl.BlockSpec((tm, d), lambda i: (i, 0)),
        out_shape=jax.ShapeDtypeStruct((t, d), jnp.bfloat16),
        compiler_params=_cparams("parallel"),
        name="rmsnorm_cast",
    )(x, g.reshape(1, d))


def _matmul_kernel(a_ref, w_ref, o_ref):
    o_ref[...] = _dot(a_ref[...], w_ref[...]).astype(o_ref.dtype)


def _matmul(a, w, tn, out_dtype, name, tm=ROW_TILE):
    t, k = a.shape
    n = w.shape[1]
    return pl.pallas_call(
        _matmul_kernel,
        grid=(t // tm, n // tn),
        in_specs=[pl.BlockSpec((tm, k), lambda i, j: (i, 0)), pl.BlockSpec((k, tn), lambda i, j: (0, j))],
        out_specs=pl.BlockSpec((tm, tn), lambda i, j: (i, j)),
        out_shape=jax.ShapeDtypeStruct((t, n), out_dtype),
        compiler_params=_cparams("parallel", "arbitrary"),
        name=name,
    )(a, w)


def _matmul_resid_kernel(k_total, a_ref, w_ref, x_ref, gpost_ref, gnext_ref, xo_ref, hn_ref):
    k = pl.program_id(1)
    last = pl.num_programs(1) - 1
    tm, d = xo_ref.shape
    tk = a_ref.shape[1]
    tail = k_total % tk

    @pl.when(k == 0)
    def _():
        xo_ref[...] = jnp.zeros_like(xo_ref)

    def accumulate(masked):
        a = a_ref[...]
        if masked:
            acol = lax.broadcasted_iota(jnp.int32, a.shape, 1) < tail
            a = _bf16(jnp.where(acol, a.astype(jnp.float32), 0.0))
            wrow = lax.broadcasted_iota(jnp.int32, (tk, RESID_COLS), 0) < tail
        for n in range(d // RESID_COLS):
            cols = slice(n * RESID_COLS, (n + 1) * RESID_COLS)
            w = w_ref[:, cols]
            if masked:
                w = _bf16(jnp.where(wrow, w.astype(jnp.float32), 0.0))
            xo_ref[:, cols] += _dot(a, w)

    if tail == 0:
        accumulate(False)
    else:
        pl.when(k < last)(lambda: accumulate(False))
        pl.when(k == last)(lambda: accumulate(True))

    @pl.when(k == last)
    def _():
        for r in range(tm // RESID_ROWS):
            rows = slice(r * RESID_ROWS, (r + 1) * RESID_ROWS)
            f = xo_ref[rows, :]
            xn = x_ref[rows, :] + f * _rms_scale(f) * gpost_ref[...]
            xo_ref[rows, :] = xn
            hn_ref[rows, :] = _bf16(xn * _rms_scale(xn) * gnext_ref[...])


def _matmul_resid(a, w, x, g_post, g_next, name, tm=ROW_TILE, tk=RESID_TK):
    t, kk = a.shape
    d = w.shape[1]
    return pl.pallas_call(
        functools.partial(_matmul_resid_kernel, kk),
        grid=(t // tm, pl.cdiv(kk, tk)),
        in_specs=[
            pl.BlockSpec((tm, tk), lambda i, k: (i, k)),
            pl.BlockSpec((tk, d), lambda i, k: (k, 0)),
            pl.BlockSpec((tm, d), lambda i, k: (i, 0), pipeline_mode=pl.Buffered(1)),
            pl.BlockSpec((1, d), lambda i, k: (0, 0)),
            pl.BlockSpec((1, d), lambda i, k: (0, 0)),
        ],
        out_specs=[pl.BlockSpec((tm, d), lambda i, k: (i, 0)), pl.BlockSpec((tm, d), lambda i, k: (i, 0))],
        out_shape=[jax.ShapeDtypeStruct((t, d), jnp.float32), jax.ShapeDtypeStruct((t, d), jnp.bfloat16)],
        compiler_params=_cparams("parallel", "arbitrary"),
        name=name,
    )(a, w, x, g_post.reshape(1, d), g_next.reshape(1, d))


def _ffn_up_kernel(seq, n_col, h_ref, halo_ref, wg_ref, wv_ref, cwg_ref, cwv_ref, cbg_ref, cbv_ref, o_ref,
                   lhs_ref, ug_ref, uv_ref):
    i = pl.program_id(0)
    j = pl.program_id(1)
    tm, tn = o_ref.shape
    hr = FFN_HALO

    @pl.when(j == 0)
    def _():
        keep = ((i * tm) % seq != 0).astype(jnp.bfloat16)
        lhs_ref[:hr, :] = halo_ref[...] * keep
        lhs_ref[hr:, :] = h_ref[...]

    def matmuls(cur):
        lhs = lhs_ref[...]
        for c in range(tn // FFN_CHUNK):
            cols = slice(c * FFN_CHUNK, (c + 1) * FFN_CHUNK)
            ug_ref[cur, :, cols] = _dot(lhs, wg_ref[:, cols])
            uv_ref[cur, :, cols] = _dot(lhs, wv_ref[:, cols])

    def conv(u_ref, prev, cw_ref, cb_ref, cols, r0):
        w = u_ref[prev, r0 + hr - SUBLANES:r0 + hr + FFN_ROWS, cols]
        p1 = pltpu.roll(w, 1, 0)[SUBLANES:]
        p2 = pltpu.roll(w, 2, 0)[SUBLANES:]
        cw = cw_ref[:, cols]
        return cb_ref[:, cols] + cw[0:1] * p2 + cw[1:2] * p1 + cw[2:3] * w[SUBLANES:]

    def epilogue(prev):
        for c in range(tn // FFN_CHUNK):
            cols = slice(c * FFN_CHUNK, (c + 1) * FFN_CHUNK)
            for r0 in range(0, tm, FFN_ROWS):
                gate = conv(ug_ref, prev, cwg_ref, cbg_ref, cols, r0)
                val = conv(uv_ref, prev, cwv_ref, cbv_ref, cols, r0)
                o_ref[r0:r0 + FFN_ROWS, cols] = _bf16(_gelu_tanh(gate) * val)

    pl.when(j == 0)(lambda: matmuls(0))
    for parity in range(2):
        @pl.when((j > 0) & (j < n_col) & (j % 2 == parity))
        def _():
            epilogue(1 - parity)
            matmuls(parity)
    pl.when(j == n_col)(lambda: epilogue((n_col - 1) % 2))


def _ffn_up(hn, wg, wv, cwg, cwv, cbg, cbv, seq, tm=ROW_TILE, tn=FFN_TN):
    t, d = hn.shape
    f = wg.shape[1]
    n_col = pl.cdiv(f, tn)
    hb = tm // FFN_HALO
    wcol = lambda i, j: (0, jnp.minimum(j, n_col - 1))
    ecol = lambda i, j: (0, jnp.maximum(j - 1, 0))
    return pl.pallas_call(
        functools.partial(_ffn_up_kernel, seq, n_col),
        grid=(t // tm, n_col + 1),
        in_specs=[
            pl.BlockSpec((tm, d), lambda i, j: (i, 0)),
            pl.BlockSpec((FFN_HALO, d), lambda i, j: (jnp.maximum(i * hb - 1, 0), 0)),
            pl.BlockSpec((d, tn), wcol),
            pl.BlockSpec((d, tn), wcol),
            pl.BlockSpec((3, tn), ecol),
            pl.BlockSpec((3, tn), ecol),
            pl.BlockSpec((1, tn), ecol),
            pl.BlockSpec((1, tn), ecol),
        ],
        out_specs=pl.BlockSpec((tm, tn), lambda i, j: (i, jnp.maximum(j - 1, 0))),
        out_shape=jax.ShapeDtypeStruct((t, f), jnp.bfloat16),
        scratch_shapes=[pltpu.VMEM((tm + FFN_HALO, d), jnp.bfloat16),
                        pltpu.VMEM((2, tm + FFN_HALO, tn), jnp.float32),
                        pltpu.VMEM((2, tm + FFN_HALO, tn), jnp.float32)],
        compiler_params=_cparams("parallel", "arbitrary"),
        name="ffn_up_conv_geglu",
    )(hn, hn, wg, wv, cwg, cwv, cbg, cbv)


def _softmax_step(s_chunks, m, l, keys):
    ps, alphas, ms, ls = [], [], [], []
    for c, s in enumerate(s_chunks):
        lanes = slice(c * LANES, (c + 1) * LANES)
        m_old = m[:, lanes]
        m_new = jnp.maximum(m_old, jnp.max(s, axis=0, keepdims=True))
        alpha = jnp.exp2(m_old - m_new)
        p = jnp.exp2(s - m_new)
        ls.append(alpha * l[:, lanes] + jnp.sum(p, axis=0, keepdims=True))
        ms.append(m_new)
        alphas.append(alpha)
        p = _bf16(p)
        if s.shape[0] < keys:
            p = jnp.concatenate([p, jnp.zeros((keys - s.shape[0], LANES), jnp.bfloat16)], axis=0)
        ps.append(p)
    cat = lambda xs: jnp.concatenate(xs, axis=1)
    return cat(ps), cat(alphas), cat(ms), cat(ls)


def _split3(x):
    hi = _bf16(x)
    r = x - hi.astype(jnp.float32)
    mid = _bf16(r)
    lo = _bf16(r - mid.astype(jnp.float32))
    return hi, mid, lo


def _fox_gate_kernel(zf_ref, bf_ref, ccol_ref, crow_ref):
    seq = zf_ref.shape[0]
    blk = 256
    x = zf_ref[...] + bf_ref[...]
    logf = -(jnp.maximum(-x, 0.0) + jnp.log(1.0 + jnp.exp(-jnp.abs(x))))
    r = lax.broadcasted_iota(jnp.int32, (blk, blk), 0)
    c = lax.broadcasted_iota(jnp.int32, (blk, blk), 1)
    tri = _bf16((c <= r).astype(jnp.float32))
    carry = jnp.zeros((1, LANES), jnp.float32)
    for b in range(seq // blk):
        hi, mid, lo = _split3(logf[b * blk:(b + 1) * blk])
        cs = _dot(tri, hi) + _dot(tri, mid) + _dot(tri, lo) + carry
        ccol_ref[b * blk:(b + 1) * blk, :] = cs
        carry = cs[blk - 1:blk]
    crow_ref[...] = ccol_ref[...].T


def _fox_gate(zb, b_f, batch, seq):
    bf = jnp.zeros((1, LANES), jnp.float32).at[0, :FOX_HEADS].set(b_f)
    return pl.pallas_call(
        _fox_gate_kernel,
        grid=(batch,),
        in_specs=[pl.BlockSpec((seq, LANES), lambda b: (b, COL_GATE // LANES)),
                  pl.BlockSpec((1, LANES), lambda b: (0, 0))],
        out_specs=[pl.BlockSpec((seq, LANES), lambda b: (b, 0)), pl.BlockSpec((None, LANES, seq), lambda b: (b, 0, 0))],
        out_shape=[jax.ShapeDtypeStruct((batch * seq, LANES), jnp.float32),
                   jax.ShapeDtypeStruct((batch, LANES, seq), jnp.float32)],
        compiler_params=_cparams("parallel"),
        name="fox_gate_cumsum",
    )(zb, bf)


def _fox_attn_kernel(q_ref, k_ref, vt_ref, cq_ref, ck_ref, o_ref, acc_ref):
    qi = pl.program_id(2)
    tq = q_ref.shape[0]
    nch = tq // LANES
    qt = _bf16(q_ref[...].astype(jnp.float32).T)
    cq = cq_ref[...] * LOG2E
    acc_ref[...] = jnp.zeros_like(acc_ref)

    def step(kt, m, l, diagonal):
        off = pl.multiple_of(kt * tq, tq)
        st = _dot(k_ref[pl.ds(off, tq), :], qt)
        ck = ck_ref[pl.ds(off, tq), :] * LOG2E
        chunks = []
        for c in range(nch):
            lanes = slice(c * LANES, (c + 1) * LANES)
            if diagonal:
                rows = (c + 1) * LANES
                s = st[:rows, lanes] * QK_SCALE + (cq[:, lanes] - ck[:rows])
                krow = lax.broadcasted_iota(jnp.int32, (rows, LANES), 0)
                qcol = c * LANES + lax.broadcasted_iota(jnp.int32, (rows, LANES), 1)
                s = jnp.where(krow <= qcol, s, NEG_INF)
            else:
                s = st[:, lanes] * QK_SCALE + (cq[:, lanes] - ck)
            chunks.append(s)
        p, alpha, m, l = _softmax_step(chunks, m, l, tq)
        acc_ref[...] = acc_ref[...] * alpha + _dot(vt_ref[:, pl.ds(off, tq)], p)
        return m, l

    init = (jnp.full((1, tq), -jnp.inf, jnp.float32), jnp.zeros((1, tq), jnp.float32))
    m, l = lax.fori_loop(0, qi, lambda kt, c: step(kt, c[0], c[1], False), init)
    _, l = step(qi, m, l, True)
    o_ref[...] = (acc_ref[...] * (1.0 / l)).T


def _heads_t(z, col, heads, batch, seq):
    return z[:, col:col + heads * HEAD_DIM].reshape(batch, seq, heads, HEAD_DIM).transpose(0, 2, 3, 1)


def _fox_attn(za, ccol, crow, batch, seq, tq=FOX_TQ):
    nq = seq // tq
    ck = ccol[:, :FOX_HEADS].reshape(batch, seq, FOX_HEADS).transpose(0, 2, 1).reshape(batch, FOX_HEADS, seq, 1)
    cq = crow[:, :FOX_HEADS, :].reshape(batch, FOX_HEADS, 1, seq)
    vt = _heads_t(za, COL_VA, FOX_HEADS, batch, seq)
    cb = lambda c: c // HEAD_DIM
    return pl.pallas_call(
        _fox_attn_kernel,
        grid=(batch, FOX_HEADS, nq),
        in_specs=[
            pl.BlockSpec((tq, HEAD_DIM), lambda b, h, i: (b * nq + i, cb(COL_QA) + h)),
            pl.BlockSpec((seq, HEAD_DIM), lambda b, h, i: (b, cb(COL_KA) + h)),
            pl.BlockSpec((None, None, HEAD_DIM, seq), lambda b, h, i: (b, h, 0, 0)),
            pl.BlockSpec((None, None, 1, tq), lambda b, h, i: (b, h, 0, i)),
            pl.BlockSpec((None, None, seq, 1), lambda b, h, i: (b, h, 0, 0)),
        ],
        out_specs=pl.BlockSpec((tq, HEAD_DIM), lambda b, h, i: (b * nq + i, h)),
        out_shape=jax.ShapeDtypeStruct((batch * seq, FOX_WIDTH), jnp.float32),
        scratch_shapes=[pltpu.VMEM((HEAD_DIM, tq), jnp.float32)],
        compiler_params=_cparams("parallel", "parallel", "arbitrary"),
        name="fox_attention",
    )(za, za, vt, cq, ck)


def _s5_disc_kernel(are_ref, aim_ref, ldt_ref, bre_ref, bim_ref, bbre_ref, bbim_ref, pwre_ref, pwim_ref):
    lam_re = jnp.minimum(are_ref[...], -1e-4)
    lam_im = aim_ref[...]
    dt = jnp.exp(ldt_ref[...])
    mag = jnp.exp(lam_re * dt)
    ang = lam_im * dt
    lb_re, lb_im = mag * jnp.cos(ang), mag * jnp.sin(ang)
    den = lam_re * lam_re + lam_im * lam_im
    nr, ni = lb_re - 1.0, lb_im
    coef_re = (nr * lam_re + ni * lam_im) / den
    coef_im = (ni * lam_re - nr * lam_im) / den
    b_re, b_im = bre_ref[...], bim_ref[...]
    bbre_ref[...] = coef_re * b_re - coef_im * b_im
    bbim_ref[...] = coef_re * b_im + coef_im * b_re
    p_re, p_im = lb_re, lb_im
    for i in range(SUBLANES):
        pwre_ref[i:i + 1, :] = p_re
        pwim_ref[i:i + 1, :] = p_im
        p_re, p_im = p_re * lb_re - p_im * lb_im, p_re * lb_im + p_im * lb_re


def _s5_discretize(a_re, a_im, log_dt, b_re, b_im):
    l, g, p = a_re.shape
    h = b_re.shape[-1]
    n = l * g * p
    row = lambda a: a.reshape(1, n)
    ldt = jnp.broadcast_to(log_dt[:, :, None], (l, g, p))
    bt = lambda b: b.reshape(n, h).T
    full = lambda r: pl.BlockSpec((r, n), lambda: (0, 0))
    bbre, bbim, pwre, pwim = pl.pallas_call(
        _s5_disc_kernel,
        in_specs=[full(1), full(1), full(1), full(h), full(h)],
        out_specs=[full(h), full(h), full(SUBLANES), full(SUBLANES)],
        out_shape=[jax.ShapeDtypeStruct((h, n), jnp.float32)] * 2 + [jax.ShapeDtypeStruct((SUBLANES, n), jnp.float32)] * 2,
        compiler_params=pltpu.CompilerParams(vmem_limit_bytes=VMEM_LIMIT),
        name="s5_discretize",
    )(row(a_re), row(a_im), row(ldt), bt(b_re), bt(b_im))
    unb = lambda b: b.T.reshape(l, g, p, h)
    unp = lambda q: q.reshape(SUBLANES, l, g * p).transpose(1, 0, 2)
    return unb(bbre), unb(bbim), unp(pwre), unp(pwim)


def _s5_scan_kernel(u_ref, bre_ref, bim_ref, cre_ref, cim_ref, d_ref, pwre_ref, pwim_ref, y_ref, hre_ref, him_ref):
    seq = u_ref.shape[0]
    u = u_ref[...]
    ub = _bf16(u)
    hre_ref[...] = _dot(ub, bre_ref[...])
    him_ref[...] = _dot(ub, bim_ref[...])

    pw_re, pw_im = pwre_ref[...], pwim_ref[...]
    sub = lax.broadcasted_iota(jnp.int32, pw_re.shape, 0)

    def shifted(i, k):
        m = sub >= k
        return (jnp.where(m, pw_re[i:i + 1], 0.0), jnp.where(m, pw_im[i:i + 1], 0.0))

    steps = [(1,) + shifted(0, 1), (2,) + shifted(1, 2), (4,) + shifted(3, 4)]

    def tile(j, carry):
        c_re, c_im = carry
        off = pl.multiple_of(j * SUBLANES, SUBLANES)
        x_re = hre_ref[pl.ds(off, SUBLANES), :]
        x_im = him_ref[pl.ds(off, SUBLANES), :]
        for k, a_re, a_im in steps:
            r_re, r_im = pltpu.roll(x_re, k, 0), pltpu.roll(x_im, k, 0)
            x_re, x_im = (x_re + a_re * r_re - a_im * r_im, x_im + a_re * r_im + a_im * r_re)
        x_re, x_im = (x_re + pw_re * c_re - pw_im * c_im, x_im + pw_re * c_im + pw_im * c_re)
        hre_ref[pl.ds(off, SUBLANES), :] = x_re
        him_ref[pl.ds(off, SUBLANES), :] = x_im
        return x_re[SUBLANES - 1:SUBLANES], x_im[SUBLANES - 1:SUBLANES]

    zero = jnp.zeros((1, pw_re.shape[1]), jnp.float32)
    lax.fori_loop(0, seq // SUBLANES, tile, (zero, zero))

    y = _dot(_bf16(hre_ref[...]), cre_ref[...]) - _dot(_bf16(him_ref[...]), cim_ref[...]) + d_ref[...] * u
    y_ref[...] = _gelu_tanh(y)


def _s5_scan(zb, b_big_re, b_big_im, c_big_re, c_big_im, d_skip, pw_re, pw_im, batch, seq):
    w = S5_SLAB_STATE
    return pl.pallas_call(
        _s5_scan_kernel,
        grid=(batch, N_S5_SLABS),
        in_specs=[
            pl.BlockSpec((seq, S5_SLAB), lambda b, j: (b, COL_UB // S5_SLAB + j)),
            pl.BlockSpec((None, S5_SLAB, w), lambda b, j: (j, 0, 0)),
            pl.BlockSpec((None, S5_SLAB, w), lambda b, j: (j, 0, 0)),
            pl.BlockSpec((None, w, S5_SLAB), lambda b, j: (j, 0, 0)),
            pl.BlockSpec((None, w, S5_SLAB), lambda b, j: (j, 0, 0)),
            pl.BlockSpec((1, S5_SLAB), lambda b, j: (0, j)),
            pl.BlockSpec((SUBLANES, w), lambda b, j: (0, j)),
            pl.BlockSpec((SUBLANES, w), lambda b, j: (0, j)),
        ],
        out_specs=pl.BlockSpec((seq, S5_SLAB), lambda b, j: (b, j)),
        out_shape=jax.ShapeDtypeStruct((batch * seq, S5_WIDTH), jnp.float32),
        scratch_shapes=[pltpu.VMEM((seq, w), jnp.float32), pltpu.VMEM((seq, w), jnp.float32)],
        compiler_params=_cparams("parallel", "arbitrary"),
        name="s5_scan",
    )(zb, b_big_re, b_big_im, c_big_re, c_big_im, d_skip.reshape(1, S5_WIDTH), pw_re, pw_im)


def _s5_glu_kernel(y_ref, w_ref, g_ref, o_ref):
    y = y_ref[...]
    o = y * _sigmoid(_dot(_bf16(y), w_ref[...]))
    o_ref[...] = _bf16(o * _rms_scale(o) * g_ref[...])


def _s5_glu(y, w_glu, g_out, tm=ROW_TILE):
    t, d = y.shape
    return pl.pallas_call(
        _s5_glu_kernel,
        grid=(t // tm,),
        in_specs=[pl.BlockSpec((tm, d), lambda i: (i, 0)), pl.BlockSpec((d, d), lambda i: (0, 0)),
                  pl.BlockSpec((1, d), lambda i: (0, 0))],
        out_specs=pl.BlockSpec((tm, d), lambda i: (i, 0)),
        out_shape=jax.ShapeDtypeStruct((t, d), jnp.bfloat16),
        compiler_params=_cparams("parallel"),
        name="s5_glu_norm",
    )(y, w_glu, g_out.reshape(1, d))


def _t5_bucket(dist):
    n = jnp.maximum(dist, 0)
    nf = jnp.maximum(n, 1).astype(jnp.float32)
    large = REL_EXACT + (jnp.log(nf / REL_EXACT) / math.log(REL_MAX_DIST / REL_EXACT)
                         * (REL_BUCKETS - REL_EXACT)).astype(jnp.int32)
    return jnp.where(n < REL_EXACT, n, jnp.minimum(large, REL_BUCKETS - 1))


def _lookup(rel_ref, bucket, h):
    out = jnp.zeros(bucket.shape, jnp.float32)
    for b in range(REL_BUCKETS):
        out = jnp.where(bucket == b, rel_ref[b, h] * LOG2E, out)
    return out


def _bias_cmp_kernel(rel_ref, o_ref):
    qi = pl.program_id(0)
    n = lax.broadcasted_iota(jnp.int32, (N_CMP_PAD, TQ), 0)
    t = qi * TQ + lax.broadcasted_iota(jnp.int32, (N_CMP_PAD, TQ), 1)
    dist = t - (n * CMP_STRIDE + CMP_BLOCK - 1)
    valid = (dist >= 0) & (n < N_CMP_PAD - 1)
    bucket = _t5_bucket(dist)
    for h in range(NSA_HEADS):
        o_ref[h] = jnp.where(valid, _lookup(rel_ref, bucket, h), NEG_INF)


def _bias_tile_kernel(rel_ref, o_ref):
    j = lax.broadcasted_iota(jnp.int32, (TQ, TQ), 0)
    i = lax.broadcasted_iota(jnp.int32, (TQ, TQ), 1)
    b0 = _t5_bucket(i - j)
    b1 = _t5_bucket(i - j + TQ)
    for h in range(NSA_HEADS):
        far = jnp.full((TQ, TQ), rel_ref[REL_BUCKETS - 1, h] * LOG2E, jnp.float32)
        o_ref[h, 0] = jnp.where(i >= j, _lookup(rel_ref, b0, h), NEG_INF)
        o_ref[h, 1] = _lookup(rel_ref, b1, h)
        o_ref[h, BIAS_TILE_FAR] = far
        o_ref[h, BIAS_TILE_EDGE] = jnp.where(j > i, far, NEG_INF)
        o_ref[h, BIAS_TILE_MASKED] = jnp.full((TQ, TQ), NEG_INF, jnp.float32)


def _nsa_bias_tables(rel_bias, seq):
    smem = pl.BlockSpec(memory_space=pltpu.SMEM)
    nq = seq // TQ
    bias_c = pl.pallas_call(
        _bias_cmp_kernel,
        grid=(nq,),
        in_specs=[smem],
        out_specs=pl.BlockSpec((NSA_HEADS, None, N_CMP_PAD, TQ), lambda i: (0, i, 0, 0)),
        out_shape=jax.ShapeDtypeStruct((NSA_HEADS, nq, N_CMP_PAD, TQ), jnp.float32),
        compiler_params=_cparams("parallel"),
        name="nsa_bias_cmp",
    )(rel_bias)
    bias_t = pl.pallas_call(
        _bias_tile_kernel,
        in_specs=[smem],
        out_specs=pl.BlockSpec((NSA_HEADS, N_BIAS_TILES, TQ, TQ), lambda: (0, 0, 0, 0)),
        out_shape=jax.ShapeDtypeStruct((NSA_HEADS, N_BIAS_TILES, TQ, TQ), jnp.float32),
        compiler_params=pltpu.CompilerParams(vmem_limit_bytes=VMEM_LIMIT),
        name="nsa_bias_tiles",
    )(rel_bias)
    return bias_c, bias_t


def _nsa_compress_kernel(transpose_out, x_ref, pe_ref, w_ref, o_ref):
    half = CMP_STRIDE * HEAD_DIM
    x = x_ref[...]
    pe = pe_ref[...]
    top = _dot(_bf16(x + pe[:, :half]), w_ref[:half, :])
    bot = _dot(_bf16(x + pe[:, half:]), w_ref[half:, :])
    n = x.shape[0]
    out = top + pltpu.roll(bot, n - 1, 0)
    o_ref[...] = _bf16(out.T if transpose_out else out)


def _nsa_compress(zb, col, pe, w, batch, seq, transpose_out):
    nchunk = seq // CMP_STRIDE
    half = CMP_STRIDE * HEAD_DIM
    x = zb[:, col:col + NSA_KV_WIDTH].reshape(batch, nchunk, CMP_STRIDE, NSA_KV_HEADS, HEAD_DIM)
    x = x.transpose(0, 3, 1, 2, 4).reshape(batch, NSA_KV_HEADS, nchunk, half)
    return pl.pallas_call(
        functools.partial(_nsa_compress_kernel, transpose_out),
        grid=(batch, NSA_KV_HEADS),
        in_specs=[pl.BlockSpec((None, None, nchunk, half), lambda b, g: (b, g, 0, 0)),
                  pl.BlockSpec((1, 2 * half), lambda b, g: (0, 0)),
                  pl.BlockSpec((2 * half, HEAD_DIM), lambda b, g: (0, 0))],
        out_specs=pl.BlockSpec((None, None, nchunk, HEAD_DIM), lambda b, g: (b, g, 0, 0)),
        out_shape=jax.ShapeDtypeStruct((batch, NSA_KV_HEADS, nchunk, HEAD_DIM), jnp.bfloat16),
        compiler_params=_cparams("parallel", "parallel"),
        name="nsa_compress",
    )(x, pe.reshape(1, 2 * half), _bf16(w.reshape(2 * half, HEAD_DIM)))


def _nsa_attn_kernel(q_ref, kc_ref, vct_ref, ks_ref, vst_ref, kw_ref, vwt_ref, zg_ref, bc_ref, bt_ref,
                     ovl_ref, exp_ref, o_ref, qt_ref, acc_ref, ocmp_ref, oslc_ref):
    qi = pl.program_id(2)
    r4 = NSA_REP
    q = q_ref[...].astype(jnp.float32)
    for r in range(r4):
        qt_ref[:, r * TQ:(r + 1) * TQ] = _bf16(q[:, r * HEAD_DIM:(r + 1) * HEAD_DIM].T)
    qt = qt_ref[...]

    st = _dot(kc_ref[...], qt)
    psum = jnp.zeros((N_CMP_PAD, TQ), jnp.float32)
    ps = []
    for r in range(r4):
        bias = bc_ref[r]
        valid = bias > 0.5 * NEG_INF
        s = jnp.where(valid, st[:, r * TQ:(r + 1) * TQ] * QK_SCALE + bias, NEG_INF)
        e = jnp.exp2(s - jnp.max(s, axis=0, keepdims=True))
        p = jnp.where(valid, e * (1.0 / jnp.sum(e, axis=0, keepdims=True)), 0.0)
        psum = psum + p
        ps.append(_bf16(p))
    ocmp_ref[...] = _dot(vct_ref[...], jnp.concatenate(ps, axis=1))

    imp = _dot(ovl_ref[...], _bf16(psum))
    jb = lax.broadcasted_iota(jnp.int32, (N_SLC, TQ), 0)
    tq = qi * TQ + lax.broadcasted_iota(jnp.int32, (N_SLC, TQ), 1)
    cur = tq // SLC_BLOCK
    forced = ((jb == 0) | (jb == cur) | (jb == cur - 1)).astype(jnp.float32)
    score = jnp.where(jb * SLC_BLOCK <= tq, imp + FORCE_BONUS * forced, NEG_INF)
    rank = jnp.zeros((N_SLC, TQ), jnp.float32)
    for i in range(N_SLC):
        si = score[i:i + 1]
        beats = (si > score) | ((si == score) & (jb > i))
        rank = rank + beats.astype(jnp.float32)
    sel = _bf16((rank < SLC_TOPN).astype(jnp.float32))

    def logits(k_ref, first_tile, n_tiles, tile_of_dist):
        off = pl.multiple_of(first_tile * TQ, TQ)
        st = _dot(k_ref[pl.ds(off, n_tiles * TQ), :], qt)
        tiles = [tile_of_dist(qi - (first_tile + u)) for u in range(n_tiles)]
        out = []
        for r in range(r4):
            bias = jnp.concatenate([bt_ref[r, tl] for tl in tiles], axis=0)
            out.append(st[:, r * TQ:(r + 1) * TQ] * QK_SCALE + bias)
        return off, out

    slc_tile = lambda d: jnp.where(d < 0, BIAS_TILE_MASKED, jnp.minimum(d, BIAS_TILE_FAR))
    keys = SLC_GROUP * TQ
    acc_ref[...] = jnp.zeros_like(acc_ref)

    def slc_body(it, carry):
        m, l = carry
        off, chunks = logits(ks_ref, it * SLC_GROUP, SLC_GROUP, slc_tile)
        selb = (_dot(exp_ref[pl.ds(off, keys), :], sel) - 1.0) * (-NEG_INF)
        p, alpha, m, l = _softmax_step([s + selb for s in chunks], m, l, keys)
        acc_ref[...] = acc_ref[...] * alpha + _dot(vst_ref[:, pl.ds(off, keys)], p)
        return m, l

    init = (jnp.full((1, r4 * TQ), -jnp.inf, jnp.float32), jnp.zeros((1, r4 * TQ), jnp.float32))
    _, l = lax.fori_loop(0, qi // SLC_GROUP + 1, slc_body, init)
    oslc_ref[...] = acc_ref[...] * (1.0 / l)

    far_d = WINDOW // TQ
    win_tile = lambda d: jnp.where(d < 0, BIAS_TILE_MASKED, jnp.where(
        d < BIAS_TILE_FAR, d, jnp.where(d < far_d, BIAS_TILE_FAR, BIAS_TILE_EDGE)))
    off, chunks = logits(kw_ref, jnp.maximum(qi - far_d, 0), WIN_TILES, win_tile)
    p, _, _, l = _softmax_step(chunks, init[0], init[1], WIN_TILES * TQ)
    acc_ref[...] = _dot(vwt_ref[:, pl.ds(off, WIN_TILES * TQ)], p) * (1.0 / l)

    gt = _sigmoid(zg_ref[...]).T
    for r in range(r4):
        lanes = slice(r * TQ, (r + 1) * TQ)
        o = (gt[3 * r:3 * r + 1] * ocmp_ref[:, lanes] + gt[3 * r + 1:3 * r + 2] * oslc_ref[:, lanes]
             + gt[3 * r + 2:3 * r + 3] * acc_ref[:, lanes])
        o_ref[:, r * HEAD_DIM:(r + 1) * HEAD_DIM] = o.T


def _nsa_consts(seq):
    nc = np.arange(N_CMP_PAD)
    blk_start = nc * CMP_STRIDE
    blk_end = blk_start + CMP_BLOCK - 1
    sel_start = np.arange(N_SLC) * SLC_BLOCK
    ovl = ((blk_start[None, :] <= sel_start[:, None] + SLC_BLOCK - 1) & (blk_end[None, :] >= sel_start[:, None])
           & (nc[None, :] < N_CMP_PAD - 1))
    expand = (np.arange(seq)[:, None] // SLC_BLOCK) == np.arange(N_SLC)[None, :]
    return jnp.asarray(ovl, jnp.bfloat16), jnp.asarray(expand, jnp.bfloat16)


def _nsa_attn(za, z_gate, kc, vct, bias_c, bias_t, batch, seq):
    nq = seq // TQ
    ovl, expand = _nsa_consts(seq)
    vst = _heads_t(za, COL_VCS, NSA_KV_HEADS, batch, seq)
    vwt = _heads_t(za, COL_VCW, NSA_KV_HEADS, batch, seq)
    cb = lambda c: c // HEAD_DIM
    keys = lambda col: pl.BlockSpec((seq, HEAD_DIM), lambda b, g, i: (b, cb(col) + g))
    vals = pl.BlockSpec((None, None, HEAD_DIM, seq), lambda b, g, i: (b, g, 0, 0))
    cmp = pl.BlockSpec((None, None, N_CMP_PAD, HEAD_DIM), lambda b, g, i: (b, g, 0, 0))
    qw = NSA_REP * HEAD_DIM
    wide = pltpu.VMEM((HEAD_DIM, NSA_REP * TQ), jnp.float32)
    return pl.pallas_call(
        _nsa_attn_kernel,
        grid=(batch, NSA_KV_HEADS, nq),
        in_specs=[
            pl.BlockSpec((TQ, qw), lambda b, g, i: (b * nq + i, COL_QC // qw + g)),
            cmp, cmp, keys(COL_KCS), vals, keys(COL_KCW), vals,
            pl.BlockSpec((TQ, LANES), lambda b, g, i: (b * nq + i, COL_GATE // LANES + 1 + g)),
            pl.BlockSpec((NSA_REP, None, N_CMP_PAD, TQ), lambda b, g, i: (g, i, 0, 0)),
            pl.BlockSpec((NSA_REP, N_BIAS_TILES, TQ, TQ), lambda b, g, i: (g, 0, 0, 0)),
            pl.BlockSpec((N_SLC, N_CMP_PAD), lambda b, g, i: (0, 0)),
            pl.BlockSpec((seq, N_SLC), lambda b, g, i: (0, 0)),
        ],
        out_specs=pl.BlockSpec((TQ, qw), lambda b, g, i: (b * nq + i, g)),
        out_shape=jax.ShapeDtypeStruct((batch * seq, NSA_WIDTH), jnp.float32),
        scratch_shapes=[pltpu.VMEM((HEAD_DIM, NSA_REP * TQ), jnp.bfloat16), wide, wide, wide],
        compiler_params=_cparams("parallel", "parallel", "arbitrary"),
        name="nsa_attention",
    )(za, kc, vct, za, vst, za, vwt, z_gate, bias_c, bias_t, ovl, expand)


def _mixnorm_kernel(oa_ref, ob_ref, oc_ref, ga_ref, gc_ref, o_ref):
    oa = oa_ref[...]
    oc = oc_ref[...]
    o_ref[:, :FOX_WIDTH] = _bf16(oa * _rms_scale(oa) * ga_ref[...])
    o_ref[:, FOX_WIDTH:FOX_WIDTH + S5_WIDTH] = ob_ref[...]
    o_ref[:, FOX_WIDTH + S5_WIDTH:] = _bf16(oc * _rms_scale(oc) * gc_ref[...])


def _mixnorm(o_a, o_b, o_c, g_a, g_c, tm=ROW_TILE):
    t = o_a.shape[0]
    row = lambda w: pl.BlockSpec((tm, w), lambda i: (i, 0))
    vec = lambda w: pl.BlockSpec((1, w), lambda i: (0, 0))
    return pl.pallas_call(
        _mixnorm_kernel,
        grid=(t // tm,),
        in_specs=[row(FOX_WIDTH), row(S5_WIDTH), row(NSA_WIDTH), vec(FOX_WIDTH), vec(NSA_WIDTH)],
        out_specs=row(D_MODEL),
        out_shape=jax.ShapeDtypeStruct((t, D_MODEL), jnp.bfloat16),
        compiler_params=_cparams("parallel"),
        name="mixer_norm_concat",
    )(o_a, o_b, o_c, g_a.reshape(1, -1), g_c.reshape(1, -1))


def _split_w_in(w):
    f, kv = FOX_WIDTH, NSA_KV_WIDTH
    o = 0
    qa, o = w[:, o:o + f], o + f
    ka, o = w[:, o:o + f], o + f
    va, o = w[:, o:o + f], o + f
    zf, o = w[:, o:o + FOX_HEADS], o + FOX_HEADS
    ub, o = w[:, o:o + S5_WIDTH], o + S5_WIDTH
    qc, o = w[:, o:o + NSA_WIDTH], o + NSA_WIDTH
    kcc_vcc, o = w[:, o:o + 2 * kv], o + 2 * kv
    rest, o = w[:, o:o + 4 * kv], o + 4 * kv
    zg = w[:, o:o + 3 * NSA_HEADS]
    w_attn = _bf16(jnp.concatenate([qa, ka, va, qc, rest], axis=1))
    per_g = 3 * NSA_REP
    gates = [jnp.pad(zf, ((0, 0), (0, LANES - FOX_HEADS)))]
    for g in range(NSA_KV_HEADS):
        gates.append(jnp.pad(zg[:, g * per_g:(g + 1) * per_g], ((0, 0), (0, LANES - per_g))))
    w_f32 = _bf16(jnp.concatenate([ub, kcc_vcc] + gates, axis=1))
    return w_attn, w_f32


def _block_diag(blocks):
    j, n, a, b = blocks.shape
    eye = jnp.eye(n, dtype=blocks.dtype)
    return (blocks[:, :, :, None, :] * eye[None, :, None, :, None]).reshape(j, n * a, n * b)


def _s5_weights(bb_re, bb_im, c_re, c_im):
    n = S5_SLAB // S5_GROUP_CH
    layers = bb_re.shape[0]

    def slabs(m, a, b):
        d = _block_diag(m.reshape(layers * N_S5_SLABS, n, a, b))
        return _bf16(d.reshape(layers, N_S5_SLABS, n * a, n * b))

    inp = lambda m: slabs(m.transpose(0, 1, 3, 2), S5_GROUP_CH, S5_STATE)
    out = lambda m: slabs(m.transpose(0, 1, 3, 2), S5_STATE, S5_GROUP_CH)
    return inp(bb_re), inp(bb_im), out(c_re), out(c_im)


def kernel(x, w_in, b_forget, s5_a_re, s5_a_im, s5_log_dt, s5_b_re, s5_b_im, s5_c_re, s5_c_im, s5_d, s5_w_glu,
           cmp_pe_k, cmp_pe_v, cmp_w_k, cmp_w_v, rel_bias, g_out_fox, g_out_s5, g_out_nsa, w_out, g_pre_mix,
           g_post_mix, g_pre_ffn, g_post_ffn, w_up, conv_w, conv_b, w_down):
    batch, seq, d = x.shape
    t = batch * seq
    xf = x.reshape(t, d)

    bias_c, bias_t = _nsa_bias_tables(rel_bias, seq)
    bb_re, bb_im, pw_re, pw_im = _s5_discretize(s5_a_re, s5_a_im, s5_log_dt, s5_b_re, s5_b_im)
    s5w = _s5_weights(bb_re, bb_im, s5_c_re, s5_c_im)

    hn = _rmsnorm_cast(xf, g_pre_mix[0])
    for l in range(DEPTH):
        w_attn, w_f32 = _split_w_in(w_in[l])
        za = _matmul(hn, w_attn, 1024, jnp.bfloat16, "in_proj_attn")
        zb = _matmul(hn, w_f32, N_F32 // 3, jnp.float32, "in_proj_f32")

        ccol, crow = _fox_gate(zb, b_forget[l], batch, seq)
        o_a = _fox_attn(za, ccol, crow, batch, seq)

        y_b = _s5_scan(zb, *(w[l] for w in s5w), s5_d[l], pw_re[l], pw_im[l], batch, seq)
        o_b = _s5_glu(y_b, _bf16(s5_w_glu[l]), g_out_s5[l])

        kc = _nsa_compress(zb, COL_KCC, cmp_pe_k[l], cmp_w_k[l], batch, seq, False)
        vct = _nsa_compress(zb, COL_VCC, cmp_pe_v[l], cmp_w_v[l], batch, seq, True)
        o_c = _nsa_attn(za, zb, kc, vct, bias_c, bias_t, batch, seq)

        mixed = _mixnorm(o_a, o_b, o_c, g_out_fox[l], g_out_nsa[l])
        xf, hn = _matmul_resid(mixed, _bf16(w_out[l]), xf, g_post_mix[l], g_pre_ffn[l], "out_proj")

        wg, wv = _bf16(w_up[l][:, :D_FF]), _bf16(w_up[l][:, D_FF:])
        cwg, cwv = conv_w[l][:, :D_FF], conv_w[l][:, D_FF:]
        cbg, cbv = conv_b[l][None, :D_FF], conv_b[l][None, D_FF:]
        act = _ffn_up(hn, wg, wv, cwg, cwv, cbg, cbv, seq)
        g_next = g_pre_mix[l + 1] if l + 1 < DEPTH else jnp.ones((d,), jnp.float32)
        xf, hn = _matmul_resid(act, _bf16(w_down[l]), xf, g_post_ffn[l], g_next, "ffn_down")
    return xf.reshape(batch, seq, d)
```

```python
import functools
import math

import numpy as np
import jax
import jax.numpy as jnp
from jax import lax
from jax.experimental import pallas as pl
from jax.experimental.pallas import tpu as pltpu

D_MODEL = 4096
DEPTH = 4
HEAD_DIM = 128
FOX_WIDTH = 1024
FOX_HEADS = 8
S5_WIDTH = 1024
S5_GROUP_CH = 16
S5_GROUPS = 64
S5_STATE = 64
NSA_WIDTH = 2048
NSA_HEADS = 16
NSA_KV_HEADS = 4
NSA_REP = 4
NSA_KV_WIDTH = 512
CMP_BLOCK = 32
CMP_STRIDE = 16
SLC_BLOCK = 64
SLC_TOPN = 16
WINDOW = 512
FORCE_BONUS = 1e3
REL_BUCKETS = 32
REL_EXACT = 16
REL_MAX_DIST = 128
D_FF = 11008
RMS_EPS = 1e-6
NEG_INF = -1e30
LOG2E = 1.4426950408889634
QK_SCALE = HEAD_DIM ** -0.5 * LOG2E

LANES = 128
SUBLANES = 8
VMEM_LIMIT = 56 * 1024 * 1024

S5_SLAB = 256
S5_SLAB_STATE = S5_SLAB // S5_GROUP_CH * S5_STATE
N_S5_SLABS = S5_WIDTH // S5_SLAB
COL_QA, COL_KA, COL_VA, COL_QC, COL_KCS, COL_VCS, COL_KCW, COL_VCW = 0, 1024, 2048, 3072, 5120, 5632, 6144, 6656
N_ATTN = 7168
COL_UB, COL_KCC, COL_VCC, COL_GATE = 0, 1024, 1536, 2048
N_GATE = LANES * (1 + NSA_KV_HEADS)
N_F32 = COL_GATE + N_GATE
TQ = 128
SLC_GROUP = 4
WIN_TILES = WINDOW // TQ + 1
BIAS_TILE_FAR, BIAS_TILE_EDGE, BIAS_TILE_MASKED, N_BIAS_TILES = 2, 3, 4, 5
FOX_TQ = 512
N_CMP_PAD = 128
N_SLC = 32
ROW_TILE = 512
IN_PROJ_TN = 1024
RESID_TK = 512
RESID_COLS = 512
RESID_ROWS = 128
FFN_TM = 1024
FFN_TN = 512
FFN_CHUNK = 256
FFN_ROWS = 128
FFN_HALO = 16


def _cparams(*sem):
    return pltpu.CompilerParams(dimension_semantics=sem, vmem_limit_bytes=VMEM_LIMIT)


def _bf16(x):
    return x.astype(jnp.bfloat16)


def _dot(a, b):
    return jnp.dot(a, b, preferred_element_type=jnp.float32)


def _rms_scale(x):
    return lax.rsqrt(jnp.mean(x * x, axis=-1, keepdims=True) + RMS_EPS)


def _gelu_tanh(x):
    return 0.5 * x * (1.0 + jnp.tanh(math.sqrt(2.0 / math.pi) * (x + 0.044715 * (x * x * x))))


def _sigmoid(x):
    return 1.0 / (1.0 + jnp.exp(-x))


def _rmsnorm_cast_kernel(x_ref, g_ref, o_ref):
    x = x_ref[...]
    o_ref[...] = _bf16(x * _rms_scale(x) * g_ref[...])


def _rmsnorm_cast(x, g, tm=ROW_TILE):
    t, d = x.shape
    return pl.pallas_call(
        _rmsnorm_cast_kernel,
        grid=(t // tm,),
        in_specs=[pl.BlockSpec((tm, d), lambda i: (i, 0)), pl.BlockSpec((1, d), lambda i: (0, 0))],
        out_specs=pl.BlockSpec((tm, d), lambda i: (i, 0)),
        out_shape=jax.ShapeDtypeStruct((t, d), jnp.bfloat16),
        compiler_params=_cparams("parallel"),
        name="rmsnorm_cast",
    )(x, g.reshape(1, d))


def _matmul_kernel(a_ref, w_ref, o_ref):
    o_ref[...] = _dot(a_ref[...], w_ref[...]).astype(o_ref.dtype)


def _matmul(a, w_tiles, out_dtype, name, tm=ROW_TILE):
    t, k = a.shape
    n_tiles, _, tn = w_tiles.shape
    return pl.pallas_call(
        _matmul_kernel,
        grid=(t // tm, n_tiles),
        in_specs=[pl.BlockSpec((tm, k), lambda i, j: (i, 0)), pl.BlockSpec((None, k, tn), lambda i, j: (j, 0, 0))],
        out_specs=pl.BlockSpec((tm, tn), lambda i, j: (i, j)),
        out_shape=jax.ShapeDtypeStruct((t, n_tiles * tn), out_dtype),
        compiler_params=_cparams("parallel", "arbitrary"),
        name=name,
    )(a, w_tiles)


def _matmul_resid_kernel(k_total, a_ref, w0_ref, w1_ref, x_ref, gpost_ref, gnext_ref, xo_ref, hn_ref):
    k = pl.program_id(1)
    last = pl.num_programs(1) - 1
    tm, d = xo_ref.shape
    tk = a_ref.shape[1]
    tail = k_total % tk
    half_chunks = d // 2 // RESID_COLS

    @pl.when(k == 0)
    def _():
        xo_ref[...] = jnp.zeros_like(xo_ref)

    def accumulate(masked):
        a = a_ref[...]
        if masked:
            acol = lax.broadcasted_iota(jnp.int32, a.shape, 1) < tail
            a = _bf16(jnp.where(acol, a.astype(jnp.float32), 0.0))
            wrow = lax.broadcasted_iota(jnp.int32, (tk, RESID_COLS), 0) < tail
        for n in range(d // RESID_COLS):
            cols = slice(n * RESID_COLS, (n + 1) * RESID_COLS)
            w_ref, m = (w0_ref, n) if n < half_chunks else (w1_ref, n - half_chunks)
            w = w_ref[:, m * RESID_COLS:(m + 1) * RESID_COLS]
            if masked:
                w = _bf16(jnp.where(wrow, w.astype(jnp.float32), 0.0))
            xo_ref[:, cols] += _dot(a, w)

    if tail == 0:
        accumulate(False)
    else:
        pl.when(k < last)(lambda: accumulate(False))
        pl.when(k == last)(lambda: accumulate(True))

    @pl.when(k == last)
    def _():
        for r in range(tm // RESID_ROWS):
            rows = slice(r * RESID_ROWS, (r + 1) * RESID_ROWS)
            f = xo_ref[rows, :]
            xn = x_ref[rows, :] + f * _rms_scale(f) * gpost_ref[...]
            xo_ref[rows, :] = xn
            hn_ref[rows, :] = _bf16(xn * _rms_scale(xn) * gnext_ref[...])


def _matmul_resid(a, w, x, g_post, g_next, name, tm=ROW_TILE, tk=RESID_TK):
    t, kk = a.shape
    d = w.shape[1]
    return pl.pallas_call(
        functools.partial(_matmul_resid_kernel, kk),
        grid=(t // tm, pl.cdiv(kk, tk)),
        in_specs=[
            pl.BlockSpec((tm, tk), lambda i, k: (i, k)),
            pl.BlockSpec((tk, d // 2), lambda i, k: (k, 0)),
            pl.BlockSpec((tk, d // 2), lambda i, k: (k, 1)),
            pl.BlockSpec((tm, d), lambda i, k: (i, 0), pipeline_mode=pl.Buffered(1)),
            pl.BlockSpec((1, d), lambda i, k: (0, 0)),
            pl.BlockSpec((1, d), lambda i, k: (0, 0)),
        ],
        out_specs=[pl.BlockSpec((tm, d), lambda i, k: (i, 0)), pl.BlockSpec((tm, d), lambda i, k: (i, 0))],
        out_shape=[jax.ShapeDtypeStruct((t, d), jnp.float32), jax.ShapeDtypeStruct((t, d), jnp.bfloat16)],
        compiler_params=_cparams("parallel", "arbitrary"),
        name=name,
    )(a, w, w, x, g_post.reshape(1, d), g_next.reshape(1, d))


def _ffn_up_kernel(seq, n_col, h_ref, halo_ref, wg_ref, wv_ref, cwg_ref, cwv_ref, cbg_ref, cbv_ref, o_ref,
                   lhs_ref, ug_ref, uv_ref):
    i = pl.program_id(0)
    j = pl.program_id(1)
    tm, tn = o_ref.shape
    hr = FFN_HALO

    @pl.when(j == 0)
    def _():
        keep = ((i * tm) % seq != 0).astype(jnp.bfloat16)
        lhs_ref[:hr, :] = halo_ref[...] * keep
        lhs_ref[hr:, :] = h_ref[...]

    def matmuls(cur):
        lhs = lhs_ref[...]
        for c in range(tn // FFN_CHUNK):
            cols = slice(c * FFN_CHUNK, (c + 1) * FFN_CHUNK)
            ug_ref[cur, :, cols] = _dot(lhs, wg_ref[:, cols])
            uv_ref[cur, :, cols] = _dot(lhs, wv_ref[:, cols])

    def conv(u_ref, prev, cw_ref, cb_ref, cols, r0):
        w = u_ref[prev, r0 + hr - SUBLANES:r0 + hr + FFN_ROWS, cols]
        p1 = pltpu.roll(w, 1, 0)[SUBLANES:]
        p2 = pltpu.roll(w, 2, 0)[SUBLANES:]
        cw = cw_ref[:, cols]
        return cb_ref[:, cols] + cw[0:1] * p2 + cw[1:2] * p1 + cw[2:3] * w[SUBLANES:]

    def epilogue(prev):
        for c in range(tn // FFN_CHUNK):
            cols = slice(c * FFN_CHUNK, (c + 1) * FFN_CHUNK)
            for r0 in range(0, tm, FFN_ROWS):
                gate = conv(ug_ref, prev, cwg_ref, cbg_ref, cols, r0)
                val = conv(uv_ref, prev, cwv_ref, cbv_ref, cols, r0)
                o_ref[r0:r0 + FFN_ROWS, cols] = _bf16(_gelu_tanh(gate) * val)

    pl.when(j == 0)(lambda: matmuls(0))
    for parity in range(2):
        @pl.when((j > 0) & (j < n_col) & (j % 2 == parity))
        def _():
            epilogue(1 - parity)
            matmuls(parity)
    pl.when(j == n_col)(lambda: epilogue((n_col - 1) % 2))


def _ffn_up(hn, wg, wv, cwg, cwv, cbg, cbv, seq, tm=FFN_TM):
    t, d = hn.shape
    n_col, _, tn = wg.shape
    f = cbg.shape[1]
    hb = tm // FFN_HALO
    wcol = lambda i, j: (jnp.minimum(j, n_col - 1), 0, 0)
    ecol = lambda i, j: (0, jnp.maximum(j - 1, 0))
    return pl.pallas_call(
        functools.partial(_ffn_up_kernel, seq, n_col),
        grid=(t // tm, n_col + 1),
        in_specs=[
            pl.BlockSpec((tm, d), lambda i, j: (i, 0), pipeline_mode=pl.Buffered(1)),
            pl.BlockSpec((FFN_HALO, d), lambda i, j: (jnp.maximum(i * hb - 1, 0), 0)),
            pl.BlockSpec((None, d, tn), wcol),
            pl.BlockSpec((None, d, tn), wcol),
            pl.BlockSpec((3, tn), ecol),
            pl.BlockSpec((3, tn), ecol),
            pl.BlockSpec((1, tn), ecol),
            pl.BlockSpec((1, tn), ecol),
        ],
        out_specs=pl.BlockSpec((tm, tn), lambda i, j: (i, jnp.maximum(j - 1, 0))),
        out_shape=jax.ShapeDtypeStruct((t, f), jnp.bfloat16),
        scratch_shapes=[pltpu.VMEM((tm + FFN_HALO, d), jnp.bfloat16),
                        pltpu.VMEM((2, tm + FFN_HALO, tn), jnp.float32),
                        pltpu.VMEM((2, tm + FFN_HALO, tn), jnp.float32)],
        compiler_params=_cparams("parallel", "arbitrary"),
        name="ffn_up_conv_geglu",
    )(hn, hn, wg, wv, cwg, cwv, cbg, cbv)


def _softmax_step(s_chunks, m, l, keys):
    ps, alphas, ms, ls = [], [], [], []
    for c, s in enumerate(s_chunks):
        lanes = slice(c * LANES, (c + 1) * LANES)
        m_old = m[:, lanes]
        m_new = jnp.maximum(m_old, jnp.max(s, axis=0, keepdims=True))
        alpha = jnp.exp2(m_old - m_new)
        p = jnp.exp2(s - m_new)
        ls.append(alpha * l[:, lanes] + jnp.sum(p, axis=0, keepdims=True))
        ms.append(m_new)
        alphas.append(alpha)
        p = _bf16(p)
        if s.shape[0] < keys:
            p = jnp.concatenate([p, jnp.zeros((keys - s.shape[0], LANES), jnp.bfloat16)], axis=0)
        ps.append(p)
    cat = lambda xs: jnp.concatenate(xs, axis=1)
    return cat(ps), cat(alphas), cat(ms), cat(ls)


def _split3(x):
    hi = _bf16(x)
    r = x - hi.astype(jnp.float32)
    mid = _bf16(r)
    lo = _bf16(r - mid.astype(jnp.float32))
    return hi, mid, lo


def _fox_gate_kernel(zf_ref, bf_ref, ccol_ref, crow_ref):
    seq = zf_ref.shape[0]
    blk = 256
    x = zf_ref[...] + bf_ref[...]
    logf = -(jnp.maximum(-x, 0.0) + jnp.log(1.0 + jnp.exp(-jnp.abs(x))))
    r = lax.broadcasted_iota(jnp.int32, (blk, blk), 0)
    c = lax.broadcasted_iota(jnp.int32, (blk, blk), 1)
    tri = _bf16((c <= r).astype(jnp.float32))
    carry = jnp.zeros((1, LANES), jnp.float32)
    for b in range(seq // blk):
        hi, mid, lo = _split3(logf[b * blk:(b + 1) * blk])
        cs = _dot(tri, hi) + _dot(tri, mid) + _dot(tri, lo) + carry
        ccol_ref[b * blk:(b + 1) * blk, :] = cs
        carry = cs[blk - 1:blk]
    crow_ref[...] = ccol_ref[...].T


def _fox_gate(zb, b_f, batch, seq):
    bf = jnp.zeros((1, LANES), jnp.float32).at[0, :FOX_HEADS].set(b_f)
    return pl.pallas_call(
        _fox_gate_kernel,
        grid=(batch,),
        in_specs=[pl.BlockSpec((seq, LANES), lambda b: (b, COL_GATE // LANES)),
                  pl.BlockSpec((1, LANES), lambda b: (0, 0))],
        out_specs=[pl.BlockSpec((seq, LANES), lambda b: (b, 0)), pl.BlockSpec((None, LANES, seq), lambda b: (b, 0, 0))],
        out_shape=[jax.ShapeDtypeStruct((batch * seq, LANES), jnp.float32),
                   jax.ShapeDtypeStruct((batch, LANES, seq), jnp.float32)],
        compiler_params=_cparams("parallel"),
        name="fox_gate_cumsum",
    )(zb, bf)


def _fox_attn_kernel(q_ref, k_ref, vt_ref, cq_ref, ck_ref, o_ref, acc_ref):
    qi = pl.program_id(2)
    tq = q_ref.shape[0]
    nch = tq // LANES
    qt = _bf16(q_ref[...].astype(jnp.float32).T)
    cq = cq_ref[...] * LOG2E
    acc_ref[...] = jnp.zeros_like(acc_ref)

    def step(kt, m, l, diagonal):
        off = pl.multiple_of(kt * tq, tq)
        st = _dot(k_ref[pl.ds(off, tq), :], qt)
        ck = ck_ref[pl.ds(off, tq), :] * LOG2E
        chunks = []
        for c in range(nch):
            lanes = slice(c * LANES, (c + 1) * LANES)
            if diagonal:
                rows = (c + 1) * LANES
                s = st[:rows, lanes] * QK_SCALE + (cq[:, lanes] - ck[:rows])
                krow = lax.broadcasted_iota(jnp.int32, (rows, LANES), 0)
                qcol = c * LANES + lax.broadcasted_iota(jnp.int32, (rows, LANES), 1)
                s = jnp.where(krow <= qcol, s, NEG_INF)
            else:
                s = st[:, lanes] * QK_SCALE + (cq[:, lanes] - ck)
            chunks.append(s)
        p, alpha, m, l = _softmax_step(chunks, m, l, tq)
        acc_ref[...] = acc_ref[...] * alpha + _dot(vt_ref[:, pl.ds(off, tq)], p)
        return m, l

    init = (jnp.full((1, tq), -jnp.inf, jnp.float32), jnp.zeros((1, tq), jnp.float32))
    m, l = lax.fori_loop(0, qi, lambda kt, c: step(kt, c[0], c[1], False), init)
    _, l = step(qi, m, l, True)
    o_ref[...] = (acc_ref[...] * (1.0 / l)).T


def _heads_t(z, col, heads, batch, seq):
    return z[:, col:col + heads * HEAD_DIM].reshape(batch, seq, heads, HEAD_DIM).transpose(0, 2, 3, 1)


def _fox_attn(za, ccol, crow, batch, seq, tq=FOX_TQ):
    nq = seq // tq
    ck = ccol[:, :FOX_HEADS].reshape(batch, seq, FOX_HEADS).transpose(0, 2, 1).reshape(batch, FOX_HEADS, seq, 1)
    cq = crow[:, :FOX_HEADS, :].reshape(batch, FOX_HEADS, 1, seq)
    vt = _heads_t(za, COL_VA, FOX_HEADS, batch, seq)
    cb = lambda c: c // HEAD_DIM
    return pl.pallas_call(
        _fox_attn_kernel,
        grid=(batch, FOX_HEADS, nq),
        in_specs=[
            pl.BlockSpec((tq, HEAD_DIM), lambda b, h, i: (b * nq + i, cb(COL_QA) + h)),
            pl.BlockSpec((seq, HEAD_DIM), lambda b, h, i: (b, cb(COL_KA) + h)),
            pl.BlockSpec((None, None, HEAD_DIM, seq), lambda b, h, i: (b, h, 0, 0)),
            pl.BlockSpec((None, None, 1, tq), lambda b, h, i: (b, h, 0, i)),
            pl.BlockSpec((None, None, seq, 1), lambda b, h, i: (b, h, 0, 0)),
        ],
        out_specs=pl.BlockSpec((tq, HEAD_DIM), lambda b, h, i: (b * nq + i, h)),
        out_shape=jax.ShapeDtypeStruct((batch * seq, FOX_WIDTH), jnp.float32),
        scratch_shapes=[pltpu.VMEM((HEAD_DIM, tq), jnp.float32)],
        compiler_params=_cparams("parallel", "parallel", "arbitrary"),
        name="fox_attention",
    )(za, za, vt, cq, ck)


def _s5_disc_kernel(are_ref, aim_ref, ldt_ref, bre_ref, bim_ref, bbre_ref, bbim_ref, pwre_ref, pwim_ref):
    lam_re = jnp.minimum(are_ref[...], -1e-4)
    lam_im = aim_ref[...]
    dt = jnp.exp(ldt_ref[...])
    mag = jnp.exp(lam_re * dt)
    ang = lam_im * dt
    lb_re, lb_im = mag * jnp.cos(ang), mag * jnp.sin(ang)
    den = lam_re * lam_re + lam_im * lam_im
    nr, ni = lb_re - 1.0, lb_im
    coef_re = (nr * lam_re + ni * lam_im) / den
    coef_im = (ni * lam_re - nr * lam_im) / den
    b_re, b_im = bre_ref[...], bim_ref[...]
    bbre_ref[...] = coef_re * b_re - coef_im * b_im
    bbim_ref[...] = coef_re * b_im + coef_im * b_re
    p_re, p_im = lb_re, lb_im
    for i in range(SUBLANES):
        pwre_ref[i:i + 1, :] = p_re
        pwim_ref[i:i + 1, :] = p_im
        p_re, p_im = p_re * lb_re - p_im * lb_im, p_re * lb_im + p_im * lb_re


def _s5_discretize(a_re, a_im, log_dt, b_re, b_im):
    l, g, p = a_re.shape
    h = b_re.shape[-1]
    n = l * g * p
    row = lambda a: a.reshape(1, n)
    ldt = jnp.broadcast_to(log_dt[:, :, None], (l, g, p))
    bt = lambda b: b.reshape(n, h).T
    full = lambda r: pl.BlockSpec((r, n), lambda: (0, 0))
    bbre, bbim, pwre, pwim = pl.pallas_call(
        _s5_disc_kernel,
        in_specs=[full(1), full(1), full(1), full(h), full(h)],
        out_specs=[full(h), full(h), full(SUBLANES), full(SUBLANES)],
        out_shape=[jax.ShapeDtypeStruct((h, n), jnp.float32)] * 2 + [jax.ShapeDtypeStruct((SUBLANES, n), jnp.float32)] * 2,
        compiler_params=pltpu.CompilerParams(vmem_limit_bytes=VMEM_LIMIT),
        name="s5_discretize",
    )(row(a_re), row(a_im), row(ldt), bt(b_re), bt(b_im))
    unb = lambda b: b.T.reshape(l, g, p, h)
    unp = lambda q: q.reshape(SUBLANES, l, g * p).transpose(1, 0, 2)
    return unb(bbre), unb(bbim), unp(pwre), unp(pwim)


def _s5_scan_kernel(u_ref, bre_ref, bim_ref, cre_ref, cim_ref, d_ref, pwre_ref, pwim_ref, y_ref, hre_ref, him_ref):
    seq = u_ref.shape[0]
    u = u_ref[...]
    ub = _bf16(u)
    hre_ref[...] = _dot(ub, bre_ref[...])
    him_ref[...] = _dot(ub, bim_ref[...])

    pw_re, pw_im = pwre_ref[...], pwim_ref[...]
    sub = lax.broadcasted_iota(jnp.int32, pw_re.shape, 0)

    def shifted(i, k):
        m = sub >= k
        return (jnp.where(m, pw_re[i:i + 1], 0.0), jnp.where(m, pw_im[i:i + 1], 0.0))

    steps = [(1,) + shifted(0, 1), (2,) + shifted(1, 2), (4,) + shifted(3, 4)]

    def tile(j, carry):
        c_re, c_im = carry
        off = pl.multiple_of(j * SUBLANES, SUBLANES)
        x_re = hre_ref[pl.ds(off, SUBLANES), :]
        x_im = him_ref[pl.ds(off, SUBLANES), :]
        for k, a_re, a_im in steps:
            r_re, r_im = pltpu.roll(x_re, k, 0), pltpu.roll(x_im, k, 0)
            x_re, x_im = (x_re + a_re * r_re - a_im * r_im, x_im + a_re * r_im + a_im * r_re)
        x_re, x_im = (x_re + pw_re * c_re - pw_im * c_im, x_im + pw_re * c_im + pw_im * c_re)
        hre_ref[pl.ds(off, SUBLANES), :] = x_re
        him_ref[pl.ds(off, SUBLANES), :] = x_im
        return x_re[SUBLANES - 1:SUBLANES], x_im[SUBLANES - 1:SUBLANES]

    zero = jnp.zeros((1, pw_re.shape[1]), jnp.float32)
    lax.fori_loop(0, seq // SUBLANES, tile, (zero, zero))

    y = _dot(_bf16(hre_ref[...]), cre_ref[...]) - _dot(_bf16(him_ref[...]), cim_ref[...]) + d_ref[...] * u
    y_ref[...] = _gelu_tanh(y)


def _s5_scan(zb, b_big_re, b_big_im, c_big_re, c_big_im, d_skip, pw_re, pw_im, batch, seq):
    w = S5_SLAB_STATE
    return pl.pallas_call(
        _s5_scan_kernel,
        grid=(batch, N_S5_SLABS),
        in_specs=[
            pl.BlockSpec((seq, S5_SLAB), lambda b, j: (b, COL_UB // S5_SLAB + j)),
            pl.BlockSpec((None, S5_SLAB, w), lambda b, j: (j, 0, 0)),
            pl.BlockSpec((None, S5_SLAB, w), lambda b, j: (j, 0, 0)),
            pl.BlockSpec((None, w, S5_SLAB), lambda b, j: (j, 0, 0)),
            pl.BlockSpec((None, w, S5_SLAB), lambda b, j: (j, 0, 0)),
            pl.BlockSpec((1, S5_SLAB), lambda b, j: (0, j)),
            pl.BlockSpec((SUBLANES, w), lambda b, j: (0, j)),
            pl.BlockSpec((SUBLANES, w), lambda b, j: (0, j)),
        ],
        out_specs=pl.BlockSpec((seq, S5_SLAB), lambda b, j: (b, j)),
        out_shape=jax.ShapeDtypeStruct((batch * seq, S5_WIDTH), jnp.float32),
        scratch_shapes=[pltpu.VMEM((seq, w), jnp.float32), pltpu.VMEM((seq, w), jnp.float32)],
        compiler_params=_cparams("parallel", "arbitrary"),
        name="s5_scan",
    )(zb, b_big_re, b_big_im, c_big_re, c_big_im, d_skip.reshape(1, S5_WIDTH), pw_re, pw_im)


def _s5_glu_kernel(y_ref, w_ref, g_ref, o_ref):
    y = y_ref[...]
    o = y * _sigmoid(_dot(_bf16(y), w_ref[...]))
    o_ref[...] = _bf16(o * _rms_scale(o) * g_ref[...])


def _s5_glu(y, w_glu, g_out, tm=ROW_TILE):
    t, d = y.shape
    return pl.pallas_call(
        _s5_glu_kernel,
        grid=(t // tm,),
        in_specs=[pl.BlockSpec((tm, d), lambda i: (i, 0)), pl.BlockSpec((d, d), lambda i: (0, 0)),
                  pl.BlockSpec((1, d), lambda i: (0, 0))],
        out_specs=pl.BlockSpec((tm, d), lambda i: (i, 0)),
        out_shape=jax.ShapeDtypeStruct((t, d), jnp.bfloat16),
        compiler_params=_cparams("parallel"),
        name="s5_glu_norm",
    )(y, w_glu, g_out.reshape(1, d))


def _t5_bucket(dist):
    n = jnp.maximum(dist, 0)
    nf = jnp.maximum(n, 1).astype(jnp.float32)
    large = REL_EXACT + (jnp.log(nf / REL_EXACT) / math.log(REL_MAX_DIST / REL_EXACT)
                         * (REL_BUCKETS - REL_EXACT)).astype(jnp.int32)
    return jnp.where(n < REL_EXACT, n, jnp.minimum(large, REL_BUCKETS - 1))


def _lookup(rel_ref, bucket, h):
    out = jnp.zeros(bucket.shape, jnp.float32)
    for b in range(REL_BUCKETS):
        out = jnp.where(bucket == b, rel_ref[b, h] * LOG2E, out)
    return out


def _bias_cmp_kernel(rel_ref, o_ref):
    qi = pl.program_id(0)
    n = lax.broadcasted_iota(jnp.int32, (N_CMP_PAD, TQ), 0)
    t = qi * TQ + lax.broadcasted_iota(jnp.int32, (N_CMP_PAD, TQ), 1)
    dist = t - (n * CMP_STRIDE + CMP_BLOCK - 1)
    valid = (dist >= 0) & (n < N_CMP_PAD - 1)
    bucket = _t5_bucket(dist)
    for h in range(NSA_HEADS):
        o_ref[h] = jnp.where(valid, _lookup(rel_ref, bucket, h), NEG_INF)


def _bias_tile_kernel(rel_ref, o_ref):
    j = lax.broadcasted_iota(jnp.int32, (TQ, TQ), 0)
    i = lax.broadcasted_iota(jnp.int32, (TQ, TQ), 1)
    b0 = _t5_bucket(i - j)
    b1 = _t5_bucket(i - j + TQ)
    for h in range(NSA_HEADS):
        far = jnp.full((TQ, TQ), rel_ref[REL_BUCKETS - 1, h] * LOG2E, jnp.float32)
        o_ref[h, 0] = jnp.where(i >= j, _lookup(rel_ref, b0, h), NEG_INF)
        o_ref[h, 1] = _lookup(rel_ref, b1, h)
        o_ref[h, BIAS_TILE_FAR] = far
        o_ref[h, BIAS_TILE_EDGE] = jnp.where(j > i, far, NEG_INF)
        o_ref[h, BIAS_TILE_MASKED] = jnp.full((TQ, TQ), NEG_INF, jnp.float32)


def _nsa_bias_tables(rel_bias, seq):
    smem = pl.BlockSpec(memory_space=pltpu.SMEM)
    nq = seq // TQ
    bias_c = pl.pallas_call(
        _bias_cmp_kernel,
        grid=(nq,),
        in_specs=[smem],
        out_specs=pl.BlockSpec((NSA_HEADS, None, N_CMP_PAD, TQ), lambda i: (0, i, 0, 0)),
        out_shape=jax.ShapeDtypeStruct((NSA_HEADS, nq, N_CMP_PAD, TQ), jnp.float32),
        compiler_params=_cparams("parallel"),
        name="nsa_bias_cmp",
    )(rel_bias)
    bias_t = pl.pallas_call(
        _bias_tile_kernel,
        in_specs=[smem],
        out_specs=pl.BlockSpec((NSA_HEADS, N_BIAS_TILES, TQ, TQ), lambda: (0, 0, 0, 0)),
        out_shape=jax.ShapeDtypeStruct((NSA_HEADS, N_BIAS_TILES, TQ, TQ), jnp.float32),
        compiler_params=pltpu.CompilerParams(vmem_limit_bytes=VMEM_LIMIT),
        name="nsa_bias_tiles",
    )(rel_bias)
    return bias_c, bias_t


def _nsa_compress_kernel(transpose_out, x_ref, pe_ref, w_ref, o_ref):
    half = CMP_STRIDE * HEAD_DIM
    x = x_ref[...]
    pe = pe_ref[...]
    top = _dot(_bf16(x + pe[:, :half]), w_ref[:half, :])
    bot = _dot(_bf16(x + pe[:, half:]), w_ref[half:, :])
    n = x.shape[0]
    out = top + pltpu.roll(bot, n - 1, 0)
    o_ref[...] = _bf16(out.T if transpose_out else out)


def _nsa_compress(zb, col, pe, w, batch, seq, transpose_out):
    nchunk = seq // CMP_STRIDE
    half = CMP_STRIDE * HEAD_DIM
    x = zb[:, col:col + NSA_KV_WIDTH].reshape(batch, nchunk, CMP_STRIDE, NSA_KV_HEADS, HEAD_DIM)
    x = x.transpose(0, 3, 1, 2, 4).reshape(batch, NSA_KV_HEADS, nchunk, half)
    return pl.pallas_call(
        functools.partial(_nsa_compress_kernel, transpose_out),
        grid=(batch, NSA_KV_HEADS),
        in_specs=[pl.BlockSpec((None, None, nchunk, half), lambda b, g: (b, g, 0, 0)),
                  pl.BlockSpec((1, 2 * half), lambda b, g: (0, 0)),
                  pl.BlockSpec((2 * half, HEAD_DIM), lambda b, g: (0, 0))],
        out_specs=pl.BlockSpec((None, None, nchunk, HEAD_DIM), lambda b, g: (b, g, 0, 0)),
        out_shape=jax.ShapeDtypeStruct((batch, NSA_KV_HEADS, nchunk, HEAD_DIM), jnp.bfloat16),
        compiler_params=_cparams("parallel", "parallel"),
        name="nsa_compress",
    )(x, pe.reshape(1, 2 * half), _bf16(w.reshape(2 * half, HEAD_DIM)))


def _nsa_attn_kernel(q_ref, kc_ref, vct_ref, ks_ref, vst_ref, kw_ref, vwt_ref, zg_ref, bc_ref, bt_ref,
                     ovl_ref, exp_ref, o_ref, qt_ref, acc_ref, ocmp_ref, oslc_ref):
    qi = pl.program_id(2)
    r4 = NSA_REP
    q = q_ref[...].astype(jnp.float32)
    for r in range(r4):
        qt_ref[:, r * TQ:(r + 1) * TQ] = _bf16(q[:, r * HEAD_DIM:(r + 1) * HEAD_DIM].T)
    qt = qt_ref[...]

    st = _dot(kc_ref[...], qt)
    psum = jnp.zeros((N_CMP_PAD, TQ), jnp.float32)
    ps = []
    for r in range(r4):
        bias = bc_ref[r]
        valid = bias > 0.5 * NEG_INF
        s = jnp.where(valid, st[:, r * TQ:(r + 1) * TQ] * QK_SCALE + bias, NEG_INF)
        e = jnp.exp2(s - jnp.max(s, axis=0, keepdims=True))
        p = jnp.where(valid, e * (1.0 / jnp.sum(e, axis=0, keepdims=True)), 0.0)
        psum = psum + p
        ps.append(_bf16(p))
    ocmp_ref[...] = _dot(vct_ref[...], jnp.concatenate(ps, axis=1))

    imp = _dot(ovl_ref[...], _bf16(psum))
    jb = lax.broadcasted_iota(jnp.int32, (N_SLC, TQ), 0)
    tq = qi * TQ + lax.broadcasted_iota(jnp.int32, (N_SLC, TQ), 1)
    cur = tq // SLC_BLOCK
    forced = ((jb == 0) | (jb == cur) | (jb == cur - 1)).astype(jnp.float32)
    score = jnp.where(jb * SLC_BLOCK <= tq, imp + FORCE_BONUS * forced, NEG_INF)
    rank = jnp.zeros((N_SLC, TQ), jnp.float32)
    for i in range(N_SLC):
        si = score[i:i + 1]
        beats = (si > score) | ((si == score) & (jb > i))
        rank = rank + beats.astype(jnp.float32)
    sel = _bf16((rank < SLC_TOPN).astype(jnp.float32))

    def logits(k_ref, first_tile, n_tiles, tile_of_dist):
        off = pl.multiple_of(first_tile * TQ, TQ)
        st = _dot(k_ref[pl.ds(off, n_tiles * TQ), :], qt)
        tiles = [tile_of_dist(qi - (first_tile + u)) for u in range(n_tiles)]
        out = []
        for r in range(r4):
            bias = jnp.concatenate([bt_ref[r, tl] for tl in tiles], axis=0)
            out.append(st[:, r * TQ:(r + 1) * TQ] * QK_SCALE + bias)
        return off, out

    slc_tile = lambda d: jnp.where(d < 0, BIAS_TILE_MASKED, jnp.minimum(d, BIAS_TILE_FAR))
    keys = SLC_GROUP * TQ
    acc_ref[...] = jnp.zeros_like(acc_ref)

    def slc_body(it, carry):
        m, l = carry
        off, chunks = logits(ks_ref, it * SLC_GROUP, SLC_GROUP, slc_tile)
        selb = (_dot(exp_ref[pl.ds(off, keys), :], sel) - 1.0) * (-NEG_INF)
        p, alpha, m, l = _softmax_step([s + selb for s in chunks], m, l, keys)
        acc_ref[...] = acc_ref[...] * alpha + _dot(vst_ref[:, pl.ds(off, keys)], p)
        return m, l

    init = (jnp.full((1, r4 * TQ), -jnp.inf, jnp.float32), jnp.zeros((1, r4 * TQ), jnp.float32))
    _, l = lax.fori_loop(0, qi // SLC_GROUP + 1, slc_body, init)
    oslc_ref[...] = acc_ref[...] * (1.0 / l)

    far_d = WINDOW // TQ
    win_tile = lambda d: jnp.where(d < 0, BIAS_TILE_MASKED, jnp.where(
        d < BIAS_TILE_FAR, d, jnp.where(d < far_d, BIAS_TILE_FAR, BIAS_TILE_EDGE)))
    off, chunks = logits(kw_ref, jnp.maximum(qi - far_d, 0), WIN_TILES, win_tile)
    p, _, _, l = _softmax_step(chunks, init[0], init[1], WIN_TILES * TQ)
    acc_ref[...] = _dot(vwt_ref[:, pl.ds(off, WIN_TILES * TQ)], p) * (1.0 / l)

    gt = _sigmoid(zg_ref[...]).T
    for r in range(r4):
        lanes = slice(r * TQ, (r + 1) * TQ)
        o = (gt[3 * r:3 * r + 1] * ocmp_ref[:, lanes] + gt[3 * r + 1:3 * r + 2] * oslc_ref[:, lanes]
             + gt[3 * r + 2:3 * r + 3] * acc_ref[:, lanes])
        o_ref[:, r * HEAD_DIM:(r + 1) * HEAD_DIM] = o.T


def _nsa_consts(seq):
    nc = np.arange(N_CMP_PAD)
    blk_start = nc * CMP_STRIDE
    blk_end = blk_start + CMP_BLOCK - 1
    sel_start = np.arange(N_SLC) * SLC_BLOCK
    ovl = ((blk_start[None, :] <= sel_start[:, None] + SLC_BLOCK - 1) & (blk_end[None, :] >= sel_start[:, None])
           & (nc[None, :] < N_CMP_PAD - 1))
    expand = (np.arange(seq)[:, None] // SLC_BLOCK) == np.arange(N_SLC)[None, :]
    return jnp.asarray(ovl, jnp.bfloat16), jnp.asarray(expand, jnp.bfloat16)


def _nsa_attn(za, z_gate, kc, vct, bias_c, bias_t, batch, seq):
    nq = seq // TQ
    ovl, expand = _nsa_consts(seq)
    vst = _heads_t(za, COL_VCS, NSA_KV_HEADS, batch, seq)
    vwt = _heads_t(za, COL_VCW, NSA_KV_HEADS, batch, seq)
    cb = lambda c: c // HEAD_DIM
    keys = lambda col: pl.BlockSpec((seq, HEAD_DIM), lambda b, g, i: (b, cb(col) + g))
    vals = pl.BlockSpec((None, None, HEAD_DIM, seq), lambda b, g, i: (b, g, 0, 0))
    cmp = pl.BlockSpec((None, None, N_CMP_PAD, HEAD_DIM), lambda b, g, i: (b, g, 0, 0))
    qw = NSA_REP * HEAD_DIM
    wide = pltpu.VMEM((HEAD_DIM, NSA_REP * TQ), jnp.float32)
    return pl.pallas_call(
        _nsa_attn_kernel,
        grid=(batch, NSA_KV_HEADS, nq),
        in_specs=[
            pl.BlockSpec((TQ, qw), lambda b, g, i: (b * nq + i, COL_QC // qw + g)),
            cmp, cmp, keys(COL_KCS), vals, keys(COL_KCW), vals,
            pl.BlockSpec((TQ, LANES), lambda b, g, i: (b * nq + i, COL_GATE // LANES + 1 + g)),
            pl.BlockSpec((NSA_REP, None, N_CMP_PAD, TQ), lambda b, g, i: (g, i, 0, 0)),
            pl.BlockSpec((NSA_REP, N_BIAS_TILES, TQ, TQ), lambda b, g, i: (g, 0, 0, 0)),
            pl.BlockSpec((N_SLC, N_CMP_PAD), lambda b, g, i: (0, 0)),
            pl.BlockSpec((seq, N_SLC), lambda b, g, i: (0, 0)),
        ],
        out_specs=pl.BlockSpec((TQ, qw), lambda b, g, i: (b * nq + i, g)),
        out_shape=jax.ShapeDtypeStruct((batch * seq, NSA_WIDTH), jnp.float32),
        scratch_shapes=[pltpu.VMEM((HEAD_DIM, NSA_REP * TQ), jnp.bfloat16), wide, wide, wide],
        compiler_params=_cparams("parallel", "parallel", "arbitrary"),
        name="nsa_attention",
    )(za, kc, vct, za, vst, za, vwt, z_gate, bias_c, bias_t, ovl, expand)


def _mixnorm_kernel(oa_ref, ob_ref, oc_ref, ga_ref, gc_ref, o_ref):
    oa = oa_ref[...]
    oc = oc_ref[...]
    o_ref[:, :FOX_WIDTH] = _bf16(oa * _rms_scale(oa) * ga_ref[...])
    o_ref[:, FOX_WIDTH:FOX_WIDTH + S5_WIDTH] = ob_ref[...]
    o_ref[:, FOX_WIDTH + S5_WIDTH:] = _bf16(oc * _rms_scale(oc) * gc_ref[...])


def _mixnorm(o_a, o_b, o_c, g_a, g_c, tm=ROW_TILE):
    t = o_a.shape[0]
    row = lambda w: pl.BlockSpec((tm, w), lambda i: (i, 0))
    vec = lambda w: pl.BlockSpec((1, w), lambda i: (0, 0))
    return pl.pallas_call(
        _mixnorm_kernel,
        grid=(t // tm,),
        in_specs=[row(FOX_WIDTH), row(S5_WIDTH), row(NSA_WIDTH), vec(FOX_WIDTH), vec(NSA_WIDTH)],
        out_specs=row(D_MODEL),
        out_shape=jax.ShapeDtypeStruct((t, D_MODEL), jnp.bfloat16),
        compiler_params=_cparams("parallel"),
        name="mixer_norm_concat",
    )(o_a, o_b, o_c, g_a.reshape(1, -1), g_c.reshape(1, -1))


def _split_w_in(w):
    f, kv = FOX_WIDTH, NSA_KV_WIDTH
    o = 0
    qa, o = w[:, o:o + f], o + f
    ka, o = w[:, o:o + f], o + f
    va, o = w[:, o:o + f], o + f
    zf, o = w[:, o:o + FOX_HEADS], o + FOX_HEADS
    ub, o = w[:, o:o + S5_WIDTH], o + S5_WIDTH
    qc, o = w[:, o:o + NSA_WIDTH], o + NSA_WIDTH
    kcc_vcc, o = w[:, o:o + 2 * kv], o + 2 * kv
    rest, o = w[:, o:o + 4 * kv], o + 4 * kv
    zg = w[:, o:o + 3 * NSA_HEADS]
    w_attn = _col_tiles(jnp.concatenate([qa, ka, va, qc, rest], axis=1), IN_PROJ_TN)
    per_g = 3 * NSA_REP
    gates = [jnp.pad(zf, ((0, 0), (0, LANES - FOX_HEADS)))]
    for g in range(NSA_KV_HEADS):
        gates.append(jnp.pad(zg[:, g * per_g:(g + 1) * per_g], ((0, 0), (0, LANES - per_g))))
    w_f32 = _col_tiles(jnp.concatenate([ub, kcc_vcc] + gates, axis=1), N_F32 // 3)
    return w_attn, w_f32


def _col_tiles(w, tn):
    k, n = w.shape
    n_tiles = pl.cdiv(n, tn)
    w = jnp.pad(w, ((0, 0), (0, n_tiles * tn - n)))
    return _bf16(w.reshape(k, n_tiles, tn).transpose(1, 0, 2))


def _block_diag(blocks):
    j, n, a, b = blocks.shape
    eye = jnp.eye(n, dtype=blocks.dtype)
    return (blocks[:, :, :, None, :] * eye[None, :, None, :, None]).reshape(j, n * a, n * b)


def _s5_weights(bb_re, bb_im, c_re, c_im):
    n = S5_SLAB // S5_GROUP_CH
    layers = bb_re.shape[0]

    def slabs(m, a, b):
        d = _block_diag(m.reshape(layers * N_S5_SLABS, n, a, b))
        return _bf16(d.reshape(layers, N_S5_SLABS, n * a, n * b))

    inp = lambda m: slabs(m.transpose(0, 1, 3, 2), S5_GROUP_CH, S5_STATE)
    out = lambda m: slabs(m.transpose(0, 1, 3, 2), S5_STATE, S5_GROUP_CH)
    return inp(bb_re), inp(bb_im), out(c_re), out(c_im)


def kernel(x, w_in, b_forget, s5_a_re, s5_a_im, s5_log_dt, s5_b_re, s5_b_im, s5_c_re, s5_c_im, s5_d, s5_w_glu,
           cmp_pe_k, cmp_pe_v, cmp_w_k, cmp_w_v, rel_bias, g_out_fox, g_out_s5, g_out_nsa, w_out, g_pre_mix,
           g_post_mix, g_pre_ffn, g_post_ffn, w_up, conv_w, conv_b, w_down):
    batch, seq, d = x.shape
    t = batch * seq
    xf = x.reshape(t, d)

    bias_c, bias_t = _nsa_bias_tables(rel_bias, seq)
    bb_re, bb_im, pw_re, pw_im = _s5_discretize(s5_a_re, s5_a_im, s5_log_dt, s5_b_re, s5_b_im)
    s5w = _s5_weights(bb_re, bb_im, s5_c_re, s5_c_im)

    hn = _rmsnorm_cast(xf, g_pre_mix[0])
    for l in range(DEPTH):
        w_attn, w_f32 = _split_w_in(w_in[l])
        za = _matmul(hn, w_attn, jnp.bfloat16, "in_proj_attn")
        zb = _matmul(hn, w_f32, jnp.float32, "in_proj_f32")

        ccol, crow = _fox_gate(zb, b_forget[l], batch, seq)
        o_a = _fox_attn(za, ccol, crow, batch, seq)

        y_b = _s5_scan(zb, *(w[l] for w in s5w), s5_d[l], pw_re[l], pw_im[l], batch, seq)
        o_b = _s5_glu(y_b, _bf16(s5_w_glu[l]), g_out_s5[l])

        kc = _nsa_compress(zb, COL_KCC, cmp_pe_k[l], cmp_w_k[l], batch, seq, False)
        vct = _nsa_compress(zb, COL_VCC, cmp_pe_v[l], cmp_w_v[l], batch, seq, True)
        o_c = _nsa_attn(za, zb, kc, vct, bias_c, bias_t, batch, seq)

        mixed = _mixnorm(o_a, o_b, o_c, g_out_fox[l], g_out_nsa[l])
        xf, hn = _matmul_resid(mixed, _bf16(w_out[l]), xf, g_post_mix[l], g_pre_ffn[l], "out_proj")

        wg, wv = _col_tiles(w_up[l][:, :D_FF], FFN_TN), _col_tiles(w_up[l][:, D_FF:], FFN_TN)
        cwg, cwv = conv_w[l][:, :D_FF], conv_w[l][:, D_FF:]
        cbg, cbv = conv_b[l][None, :D_FF], conv_b[l][None, D_FF:]
        act = _ffn_up(hn, wg, wv, cwg, cwv, cbg, cbv, seq)
        g_next = g_pre_mix[l + 1] if l + 1 < DEPTH else jnp.ones((d,), jnp.float32)
        xf, hn = _matmul_resid(act, _bf16(w_down[l]), xf, g_post_ffn[l], g_next, "ffn_down")
    return xf.reshape(batch, seq, d)
```

```python
import functools
import math

import numpy as np
import jax
import jax.numpy as jnp
from jax import lax
from jax.experimental import pallas as pl
from jax.experimental.pallas import tpu as pltpu

D_MODEL = 4096
DEPTH = 4
HEAD_DIM = 128
FOX_WIDTH = 1024
FOX_HEADS = 8
S5_WIDTH = 1024
S5_GROUP_CH = 16
S5_GROUPS = 64
S5_STATE = 64
NSA_WIDTH = 2048
NSA_HEADS = 16
NSA_KV_HEADS = 4
NSA_REP = 4
NSA_KV_WIDTH = 512
CMP_BLOCK = 32
CMP_STRIDE = 16
SLC_BLOCK = 64
SLC_TOPN = 16
WINDOW = 512
FORCE_BONUS = 1e3
REL_BUCKETS = 32
REL_EXACT = 16
REL_MAX_DIST = 128
D_FF = 11008
RMS_EPS = 1e-6
NEG_INF = -1e30
LOG2E = 1.4426950408889634
QK_SCALE = HEAD_DIM ** -0.5 * LOG2E

LANES = 128
SUBLANES = 8
VMEM_LIMIT = 60 * 1024 * 1024

S5_SLAB = 256
S5_SLAB_STATE = S5_SLAB // S5_GROUP_CH * S5_STATE
N_S5_SLABS = S5_WIDTH // S5_SLAB
COL_QA, COL_KA, COL_VA, COL_QC, COL_KCS, COL_VCS, COL_KCW, COL_VCW = 0, 1024, 2048, 3072, 5120, 5632, 6144, 6656
N_ATTN = 7168
COL_UB, COL_KCC, COL_VCC, COL_GATE = 0, 1024, 1536, 2048
N_GATE = LANES * (1 + NSA_KV_HEADS)
N_F32 = COL_GATE + N_GATE
TQ = 128
SLC_GROUP = 4
WIN_TILES = WINDOW // TQ + 1
BIAS_TILE_FAR, BIAS_TILE_EDGE, BIAS_TILE_MASKED, N_BIAS_TILES = 2, 3, 4, 5
FOX_TQ = 512
N_CMP_PAD = 128
N_SLC = 32
ROW_TILE = 512
IN_PROJ_TN = 1024
RESID_TK = 1024
RESID_COLS = 512
RESID_ROWS = 128
FFN_TN = 512
FFN_CHUNK = 256


def _cparams(*sem):
    return pltpu.CompilerParams(dimension_semantics=sem, vmem_limit_bytes=VMEM_LIMIT)


def _bf16(x):
    return x.astype(jnp.bfloat16)


def _dot(a, b):
    return jnp.dot(a, b, preferred_element_type=jnp.float32)


def _rms_scale(x):
    return lax.rsqrt(jnp.mean(x * x, axis=-1, keepdims=True) + RMS_EPS)


def _gelu_tanh(x):
    return 0.5 * x * (1.0 + jnp.tanh(math.sqrt(2.0 / math.pi) * (x + 0.044715 * (x * x * x))))


def _sigmoid(x):
    return 1.0 / (1.0 + jnp.exp(-x))


def _rmsnorm_cast_kernel(x_ref, g_ref, o_ref):
    x = x_ref[...]
    o_ref[...] = _bf16(x * _rms_scale(x) * g_ref[...])


def _rmsnorm_cast(x, g, tm=ROW_TILE):
    t, d = x.shape
    return pl.pallas_call(
        _rmsnorm_cast_kernel,
        grid=(t // tm,),
        in_specs=[pl.BlockSpec((tm, d), lambda i: (i, 0)), pl.BlockSpec((1, d), lambda i: (0, 0))],
        out_specs=pl.BlockSpec((tm, d), lambda i: (i, 0)),
        out_shape=jax.ShapeDtypeStruct((t, d), jnp.bfloat16),
        compiler_params=_cparams("parallel"),
        name="rmsnorm_cast",
    )(x, g.reshape(1, d))


def _matmul_kernel(a_ref, w_ref, o_ref):
    o_ref[...] = _dot(a_ref[...], w_ref[...]).astype(o_ref.dtype)


def _matmul(a, w, tn, out_dtype, name, tm=ROW_TILE):
    t, k = a.shape
    n = w.shape[1]
    return pl.pallas_call(
        _matmul_kernel,
        grid=(t // tm, n // tn),
        in_specs=[pl.BlockSpec((tm, k), lambda i, j: (i, 0)), pl.BlockSpec((k, tn), lambda i, j: (0, j))],
        out_specs=pl.BlockSpec((tm, tn), lambda i, j: (i, j)),
        out_shape=jax.ShapeDtypeStruct((t, n), out_dtype),
        compiler_params=_cparams("parallel", "arbitrary"),
        name=name,
    )(a, w)


def _resid_epilogue(x_ref, gpost_ref, gnext_ref, xo_ref, hn_ref):
    tm = xo_ref.shape[0]
    for r in range(tm // RESID_ROWS):
        rows = slice(r * RESID_ROWS, (r + 1) * RESID_ROWS)
        f = xo_ref[rows, :]
        xn = x_ref[rows, :] + f * _rms_scale(f) * gpost_ref[...]
        xo_ref[rows, :] = xn
        hn_ref[rows, :] = _bf16(xn * _rms_scale(xn) * gnext_ref[...])


def _matmul_resid_cols_kernel(a_ref, w_ref, x_ref, gpost_ref, gnext_ref, xo_ref, hn_ref):
    j = pl.program_id(1)
    tn = w_ref.shape[1]
    xo_ref[:, pl.ds(pl.multiple_of(j * tn, tn), tn)] = _dot(a_ref[...], w_ref[...])
    pl.when(j == pl.num_programs(1) - 1)(lambda: _resid_epilogue(x_ref, gpost_ref, gnext_ref, xo_ref, hn_ref))


def _matmul_resid_cols(a, w, x, g_post, g_next, name, tm=ROW_TILE, tn=RESID_COLS):
    t, kk = a.shape
    d = w.shape[1]
    return pl.pallas_call(
        _matmul_resid_cols_kernel,
        grid=(t // tm, d // tn),
        in_specs=[
            pl.BlockSpec((tm, kk), lambda i, j: (i, 0)),
            pl.BlockSpec((kk, tn), lambda i, j: (0, j)),
            pl.BlockSpec((tm, d), lambda i, j: (i, 0), pipeline_mode=pl.Buffered(1)),
            pl.BlockSpec((1, d), lambda i, j: (0, 0)),
            pl.BlockSpec((1, d), lambda i, j: (0, 0)),
        ],
        out_specs=[pl.BlockSpec((tm, d), lambda i, j: (i, 0)), pl.BlockSpec((tm, d), lambda i, j: (i, 0))],
        out_shape=[jax.ShapeDtypeStruct((t, d), jnp.float32), jax.ShapeDtypeStruct((t, d), jnp.bfloat16)],
        compiler_params=_cparams("parallel", "arbitrary"),
        name=name,
    )(a, w, x, g_post.reshape(1, d), g_next.reshape(1, d))


def _matmul_resid_kernel(k_total, a_ref, w_ref, x_ref, gpost_ref, gnext_ref, xo_ref, hn_ref):
    k = pl.program_id(1)
    last = pl.num_programs(1) - 1
    tm, d = xo_ref.shape
    tk = a_ref.shape[1]
    tail = k_total % tk

    @pl.when(k == 0)
    def _():
        xo_ref[...] = jnp.zeros_like(xo_ref)

    def accumulate(masked):
        a = a_ref[...]
        if masked:
            acol = lax.broadcasted_iota(jnp.int32, a.shape, 1) < tail
            a = _bf16(jnp.where(acol, a.astype(jnp.float32), 0.0))
            wrow = lax.broadcasted_iota(jnp.int32, (tk, RESID_COLS), 0) < tail
        for n in range(d // RESID_COLS):
            cols = slice(n * RESID_COLS, (n + 1) * RESID_COLS)
            w = w_ref[:, cols]
            if masked:
                w = _bf16(jnp.where(wrow, w.astype(jnp.float32), 0.0))
            xo_ref[:, cols] += _dot(a, w)

    if tail == 0:
        accumulate(False)
    else:
        pl.when(k < last)(lambda: accumulate(False))
        pl.when(k == last)(lambda: accumulate(True))

    pl.when(k == last)(lambda: _resid_epilogue(x_ref, gpost_ref, gnext_ref, xo_ref, hn_ref))


def _matmul_resid(a, w, x, g_post, g_next, name, tm=ROW_TILE, tk=RESID_TK):
    t, kk = a.shape
    d = w.shape[1]
    return pl.pallas_call(
        functools.partial(_matmul_resid_kernel, kk),
        grid=(t // tm, pl.cdiv(kk, tk)),
        in_specs=[
            pl.BlockSpec((tm, tk), lambda i, k: (i, k)),
            pl.BlockSpec((tk, d), lambda i, k: (k, 0)),
            pl.BlockSpec((tm, d), lambda i, k: (i, 0), pipeline_mode=pl.Buffered(1)),
            pl.BlockSpec((1, d), lambda i, k: (0, 0)),
            pl.BlockSpec((1, d), lambda i, k: (0, 0)),
        ],
        out_specs=[pl.BlockSpec((tm, d), lambda i, k: (i, 0)), pl.BlockSpec((tm, d), lambda i, k: (i, 0))],
        out_shape=[jax.ShapeDtypeStruct((t, d), jnp.float32), jax.ShapeDtypeStruct((t, d), jnp.bfloat16)],
        compiler_params=_cparams("parallel", "arbitrary"),
        name=name,
    )(a, w, x, g_post.reshape(1, d), g_next.reshape(1, d))


def _ffn_up_kernel(seq, h_ref, halo_ref, wg_ref, wv_ref, cwg_ref, cwv_ref, cbg_ref, cbv_ref, o_ref):
    i = pl.program_id(0)
    tm, tn = o_ref.shape
    h = h_ref[...]
    keep = ((i * tm) % seq != 0).astype(jnp.float32)
    halo = halo_ref[...]
    row = lax.broadcasted_iota(jnp.int32, (tm, 1), 0)

    def conv(w_ref, cw_ref, cb_ref, cols):
        w = w_ref[:, cols]
        u = _dot(h, w)
        uh = _dot(halo, w) * keep
        p1 = jnp.where(row == 0, uh[7:8], pltpu.roll(u, 1, 0))
        p2 = jnp.where(row == 0, uh[6:7], jnp.where(row == 1, uh[7:8], pltpu.roll(u, 2, 0)))
        cw = cw_ref[:, cols]
        return cb_ref[:, cols] + cw[0:1] * p2 + cw[1:2] * p1 + cw[2:3] * u

    for c in range(tn // FFN_CHUNK):
        cols = slice(c * FFN_CHUNK, (c + 1) * FFN_CHUNK)
        gate = conv(wg_ref, cwg_ref, cbg_ref, cols)
        val = conv(wv_ref, cwv_ref, cbv_ref, cols)
        o_ref[:, cols] = _bf16(_gelu_tanh(gate) * val)


def _ffn_up(hn, wg, wv, cwg, cwv, cbg, cbv, seq, tm=ROW_TILE, tn=FFN_TN):
    t, d = hn.shape
    f = wg.shape[1]
    hb = tm // SUBLANES
    return pl.pallas_call(
        functools.partial(_ffn_up_kernel, seq),
        grid=(t // tm, pl.cdiv(f, tn)),
        in_specs=[
            pl.BlockSpec((tm, d), lambda i, j: (i, 0)),
            pl.BlockSpec((SUBLANES, d), lambda i, j: (jnp.maximum(i * hb - 1, 0), 0)),
            pl.BlockSpec((d, tn), lambda i, j: (0, j)),
            pl.BlockSpec((d, tn), lambda i, j: (0, j)),
            pl.BlockSpec((3, tn), lambda i, j: (0, j)),
            pl.BlockSpec((3, tn), lambda i, j: (0, j)),
            pl.BlockSpec((1, tn), lambda i, j: (0, j)),
            pl.BlockSpec((1, tn), lambda i, j: (0, j)),
        ],
        out_specs=pl.BlockSpec((tm, tn), lambda i, j: (i, j)),
        out_shape=jax.ShapeDtypeStruct((t, f), jnp.bfloat16),
        compiler_params=_cparams("parallel", "arbitrary"),
        name="ffn_up_conv_geglu",
    )(hn, hn, wg, wv, cwg, cwv, cbg, cbv)


def _softmax_step(s_chunks, m, l, keys):
    ps, alphas, ms, ls = [], [], [], []
    for c, s in enumerate(s_chunks):
        lanes = slice(c * LANES, (c + 1) * LANES)
        m_old = m[:, lanes]
        m_new = jnp.maximum(m_old, jnp.max(s, axis=0, keepdims=True))
        alpha = jnp.exp2(m_old - m_new)
        p = jnp.exp2(s - m_new)
        ls.append(alpha * l[:, lanes] + jnp.sum(p, axis=0, keepdims=True))
        ms.append(m_new)
        alphas.append(alpha)
        p = _bf16(p)
        if s.shape[0] < keys:
            p = jnp.concatenate([p, jnp.zeros((keys - s.shape[0], LANES), jnp.bfloat16)], axis=0)
        ps.append(p)
    cat = lambda xs: jnp.concatenate(xs, axis=1)
    return cat(ps), cat(alphas), cat(ms), cat(ls)


def _split3(x):
    hi = _bf16(x)
    r = x - hi.astype(jnp.float32)
    mid = _bf16(r)
    lo = _bf16(r - mid.astype(jnp.float32))
    return hi, mid, lo


def _fox_gate_kernel(zf_ref, bf_ref, ccol_ref, crow_ref):
    seq = zf_ref.shape[0]
    blk = 256
    x = zf_ref[...] + bf_ref[...]
    logf = -(jnp.maximum(-x, 0.0) + jnp.log(1.0 + jnp.exp(-jnp.abs(x))))
    r = lax.broadcasted_iota(jnp.int32, (blk, blk), 0)
    c = lax.broadcasted_iota(jnp.int32, (blk, blk), 1)
    tri = _bf16((c <= r).astype(jnp.float32))
    carry = jnp.zeros((1, LANES), jnp.float32)
    for b in range(seq // blk):
        hi, mid, lo = _split3(logf[b * blk:(b + 1) * blk])
        cs = _dot(tri, hi) + _dot(tri, mid) + _dot(tri, lo) + carry
        ccol_ref[b * blk:(b + 1) * blk, :] = cs
        carry = cs[blk - 1:blk]
    crow_ref[...] = ccol_ref[...].T


def _fox_gate(zb, b_f, batch, seq):
    bf = jnp.zeros((1, LANES), jnp.float32).at[0, :FOX_HEADS].set(b_f)
    return pl.pallas_call(
        _fox_gate_kernel,
        grid=(batch,),
        in_specs=[pl.BlockSpec((seq, LANES), lambda b: (b, COL_GATE // LANES)),
                  pl.BlockSpec((1, LANES), lambda b: (0, 0))],
        out_specs=[pl.BlockSpec((seq, LANES), lambda b: (b, 0)), pl.BlockSpec((None, LANES, seq), lambda b: (b, 0, 0))],
        out_shape=[jax.ShapeDtypeStruct((batch * seq, LANES), jnp.float32),
                   jax.ShapeDtypeStruct((batch, LANES, seq), jnp.float32)],
        compiler_params=_cparams("parallel"),
        name="fox_gate_cumsum",
    )(zb, bf)


def _fox_attn_kernel(q_ref, k_ref, vt_ref, cq_ref, ck_ref, o_ref, acc_ref):
    qi = pl.program_id(2)
    tq = q_ref.shape[0]
    nch = tq // LANES
    qt = _bf16(q_ref[...].astype(jnp.float32).T)
    cq = cq_ref[...] * LOG2E
    acc_ref[...] = jnp.zeros_like(acc_ref)

    def step(kt, m, l, diagonal):
        off = pl.multiple_of(kt * tq, tq)
        st = _dot(k_ref[pl.ds(off, tq), :], qt)
        ck = ck_ref[pl.ds(off, tq), :] * LOG2E
        chunks = []
        for c in range(nch):
            lanes = slice(c * LANES, (c + 1) * LANES)
            if diagonal:
                rows = (c + 1) * LANES
                s = st[:rows, lanes] * QK_SCALE + (cq[:, lanes] - ck[:rows])
                krow = lax.broadcasted_iota(jnp.int32, (rows, LANES), 0)
                qcol = c * LANES + lax.broadcasted_iota(jnp.int32, (rows, LANES), 1)
                s = jnp.where(krow <= qcol, s, NEG_INF)
            else:
                s = st[:, lanes] * QK_SCALE + (cq[:, lanes] - ck)
            chunks.append(s)
        p, alpha, m, l = _softmax_step(chunks, m, l, tq)
        acc_ref[...] = acc_ref[...] * alpha + _dot(vt_ref[:, pl.ds(off, tq)], p)
        return m, l

    init = (jnp.full((1, tq), -jnp.inf, jnp.float32), jnp.zeros((1, tq), jnp.float32))
    m, l = lax.fori_loop(0, qi, lambda kt, c: step(kt, c[0], c[1], False), init)
    _, l = step(qi, m, l, True)
    o_ref[...] = (acc_ref[...] * (1.0 / l)).T


def _heads_t(z, col, heads, batch, seq):
    return z[:, col:col + heads * HEAD_DIM].reshape(batch, seq, heads, HEAD_DIM).transpose(0, 2, 3, 1)


def _fox_attn(za, ccol, crow, batch, seq, tq=FOX_TQ):
    nq = seq // tq
    ck = ccol[:, :FOX_HEADS].reshape(batch, seq, FOX_HEADS).transpose(0, 2, 1).reshape(batch, FOX_HEADS, seq, 1)
    cq = crow[:, :FOX_HEADS, :].reshape(batch, FOX_HEADS, 1, seq)
    vt = _heads_t(za, COL_VA, FOX_HEADS, batch, seq)
    cb = lambda c: c // HEAD_DIM
    return pl.pallas_call(
        _fox_attn_kernel,
        grid=(batch, FOX_HEADS, nq),
        in_specs=[
            pl.BlockSpec((tq, HEAD_DIM), lambda b, h, i: (b * nq + i, cb(COL_QA) + h)),
            pl.BlockSpec((seq, HEAD_DIM), lambda b, h, i: (b, cb(COL_KA) + h)),
            pl.BlockSpec((None, None, HEAD_DIM, seq), lambda b, h, i: (b, h, 0, 0)),
            pl.BlockSpec((None, None, 1, tq), lambda b, h, i: (b, h, 0, i)),
            pl.BlockSpec((None, None, seq, 1), lambda b, h, i: (b, h, 0, 0)),
        ],
        out_specs=pl.BlockSpec((tq, HEAD_DIM), lambda b, h, i: (b * nq + i, h)),
        out_shape=jax.ShapeDtypeStruct((batch * seq, FOX_WIDTH), jnp.float32),
        scratch_shapes=[pltpu.VMEM((HEAD_DIM, tq), jnp.float32)],
        compiler_params=_cparams("parallel", "parallel", "arbitrary"),
        name="fox_attention",
    )(za, za, vt, cq, ck)


def _s5_disc_kernel(are_ref, aim_ref, ldt_ref, bre_ref, bim_ref, bbre_ref, bbim_ref, pwre_ref, pwim_ref):
    lam_re = jnp.minimum(are_ref[...], -1e-4)
    lam_im = aim_ref[...]
    dt = jnp.exp(ldt_ref[...])
    mag = jnp.exp(lam_re * dt)
    ang = lam_im * dt
    lb_re, lb_im = mag * jnp.cos(ang), mag * jnp.sin(ang)
    den = lam_re * lam_re + lam_im * lam_im
    nr, ni = lb_re - 1.0, lb_im
    coef_re = (nr * lam_re + ni * lam_im) / den
    coef_im = (ni * lam_re - nr * lam_im) / den
    b_re, b_im = bre_ref[...], bim_ref[...]
    bbre_ref[...] = coef_re * b_re - coef_im * b_im
    bbim_ref[...] = coef_re * b_im + coef_im * b_re
    p_re, p_im = lb_re, lb_im
    for i in range(SUBLANES):
        pwre_ref[i:i + 1, :] = p_re
        pwim_ref[i:i + 1, :] = p_im
        p_re, p_im = p_re * lb_re - p_im * lb_im, p_re * lb_im + p_im * lb_re


def _s5_discretize(a_re, a_im, log_dt, b_re, b_im):
    l, g, p = a_re.shape
    h = b_re.shape[-1]
    n = l * g * p
    row = lambda a: a.reshape(1, n)
    ldt = jnp.broadcast_to(log_dt[:, :, None], (l, g, p))
    bt = lambda b: b.reshape(n, h).T
    full = lambda r: pl.BlockSpec((r, n), lambda: (0, 0))
    bbre, bbim, pwre, pwim = pl.pallas_call(
        _s5_disc_kernel,
        in_specs=[full(1), full(1), full(1), full(h), full(h)],
        out_specs=[full(h), full(h), full(SUBLANES), full(SUBLANES)],
        out_shape=[jax.ShapeDtypeStruct((h, n), jnp.float32)] * 2 + [jax.ShapeDtypeStruct((SUBLANES, n), jnp.float32)] * 2,
        compiler_params=pltpu.CompilerParams(vmem_limit_bytes=VMEM_LIMIT),
        name="s5_discretize",
    )(row(a_re), row(a_im), row(ldt), bt(b_re), bt(b_im))
    unb = lambda b: b.T.reshape(l, g, p, h)
    unp = lambda q: q.reshape(SUBLANES, l, g * p).transpose(1, 0, 2)
    return unb(bbre), unb(bbim), unp(pwre), unp(pwim)


def _s5_scan_kernel(u_ref, bre_ref, bim_ref, cre_ref, cim_ref, d_ref, pwre_ref, pwim_ref, y_ref, hre_ref, him_ref):
    seq = u_ref.shape[0]
    u = u_ref[...]
    ub = _bf16(u)
    hre_ref[...] = _dot(ub, bre_ref[...])
    him_ref[...] = _dot(ub, bim_ref[...])

    pw_re, pw_im = pwre_ref[...], pwim_ref[...]
    sub = lax.broadcasted_iota(jnp.int32, pw_re.shape, 0)

    def shifted(i, k):
        m = sub >= k
        return (jnp.where(m, pw_re[i:i + 1], 0.0), jnp.where(m, pw_im[i:i + 1], 0.0))

    steps = [(1,) + shifted(0, 1), (2,) + shifted(1, 2), (4,) + shifted(3, 4)]

    def tile(j, carry):
        c_re, c_im = carry
        off = pl.multiple_of(j * SUBLANES, SUBLANES)
        x_re = hre_ref[pl.ds(off, SUBLANES), :]
        x_im = him_ref[pl.ds(off, SUBLANES), :]
        for k, a_re, a_im in steps:
            r_re, r_im = pltpu.roll(x_re, k, 0), pltpu.roll(x_im, k, 0)
            x_re, x_im = (x_re + a_re * r_re - a_im * r_im, x_im + a_re * r_im + a_im * r_re)
        x_re, x_im = (x_re + pw_re * c_re - pw_im * c_im, x_im + pw_re * c_im + pw_im * c_re)
        hre_ref[pl.ds(off, SUBLANES), :] = x_re
        him_ref[pl.ds(off, SUBLANES), :] = x_im
        return x_re[SUBLANES - 1:SUBLANES], x_im[SUBLANES - 1:SUBLANES]

    zero = jnp.zeros((1, pw_re.shape[1]), jnp.float32)
    lax.fori_loop(0, seq // SUBLANES, tile, (zero, zero))

    y = _dot(_bf16(hre_ref[...]), cre_ref[...]) - _dot(_bf16(him_ref[...]), cim_ref[...]) + d_ref[...] * u
    y_ref[...] = _gelu_tanh(y)


def _s5_scan(zb, b_big_re, b_big_im, c_big_re, c_big_im, d_skip, pw_re, pw_im, batch, seq):
    w = S5_SLAB_STATE
    return pl.pallas_call(
        _s5_scan_kernel,
        grid=(batch, N_S5_SLABS),
        in_specs=[
            pl.BlockSpec((seq, S5_SLAB), lambda b, j: (b, COL_UB // S5_SLAB + j)),
            pl.BlockSpec((None, S5_SLAB, w), lambda b, j: (j, 0, 0)),
            pl.BlockSpec((None, S5_SLAB, w), lambda b, j: (j, 0, 0)),
            pl.BlockSpec((None, w, S5_SLAB), lambda b, j: (j, 0, 0)),
            pl.BlockSpec((None, w, S5_SLAB), lambda b, j: (j, 0, 0)),
            pl.BlockSpec((1, S5_SLAB), lambda b, j: (0, j)),
            pl.BlockSpec((SUBLANES, w), lambda b, j: (0, j)),
            pl.BlockSpec((SUBLANES, w), lambda b, j: (0, j)),
        ],
        out_specs=pl.BlockSpec((seq, S5_SLAB), lambda b, j: (b, j)),
        out_shape=jax.ShapeDtypeStruct((batch * seq, S5_WIDTH), jnp.float32),
        scratch_shapes=[pltpu.VMEM((seq, w), jnp.float32), pltpu.VMEM((seq, w), jnp.float32)],
        compiler_params=_cparams("parallel", "arbitrary"),
        name="s5_scan",
    )(zb, b_big_re, b_big_im, c_big_re, c_big_im, d_skip.reshape(1, S5_WIDTH), pw_re, pw_im)


def _s5_glu_kernel(y_ref, w_ref, g_ref, o_ref):
    y = y_ref[...]
    o = y * _sigmoid(_dot(_bf16(y), w_ref[...]))
    o_ref[...] = _bf16(o * _rms_scale(o) * g_ref[...])


def _s5_glu(y, w_glu, g_out, tm=ROW_TILE):
    t, d = y.shape
    return pl.pallas_call(
        _s5_glu_kernel,
        grid=(t // tm,),
        in_specs=[pl.BlockSpec((tm, d), lambda i: (i, 0)), pl.BlockSpec((d, d), lambda i: (0, 0)),
                  pl.BlockSpec((1, d), lambda i: (0, 0))],
        out_specs=pl.BlockSpec((tm, d), lambda i: (i, 0)),
        out_shape=jax.ShapeDtypeStruct((t, d), jnp.bfloat16),
        compiler_params=_cparams("parallel"),
        name="s5_glu_norm",
    )(y, w_glu, g_out.reshape(1, d))


def _t5_bucket(dist):
    n = jnp.maximum(dist, 0)
    nf = jnp.maximum(n, 1).astype(jnp.float32)
    large = REL_EXACT + (jnp.log(nf / REL_EXACT) / math.log(REL_MAX_DIST / REL_EXACT)
                         * (REL_BUCKETS - REL_EXACT)).astype(jnp.int32)
    return jnp.where(n < REL_EXACT, n, jnp.minimum(large, REL_BUCKETS - 1))


def _lookup(rel_ref, bucket, h):
    out = jnp.zeros(bucket.shape, jnp.float32)
    for b in range(REL_BUCKETS):
        out = jnp.where(bucket == b, rel_ref[b, h] * LOG2E, out)
    return out


def _bias_cmp_kernel(rel_ref, o_ref):
    qi = pl.program_id(0)
    n = lax.broadcasted_iota(jnp.int32, (N_CMP_PAD, TQ), 0)
    t = qi * TQ + lax.broadcasted_iota(jnp.int32, (N_CMP_PAD, TQ), 1)
    dist = t - (n * CMP_STRIDE + CMP_BLOCK - 1)
    valid = (dist >= 0) & (n < N_CMP_PAD - 1)
    bucket = _t5_bucket(dist)
    for h in range(NSA_HEADS):
        o_ref[h] = jnp.where(valid, _lookup(rel_ref, bucket, h), NEG_INF)


def _bias_tile_kernel(rel_ref, o_ref):
    j = lax.broadcasted_iota(jnp.int32, (TQ, TQ), 0)
    i = lax.broadcasted_iota(jnp.int32, (TQ, TQ), 1)
    b0 = _t5_bucket(i - j)
    b1 = _t5_bucket(i - j + TQ)
    for h in range(NSA_HEADS):
        far = jnp.full((TQ, TQ), rel_ref[REL_BUCKETS - 1, h] * LOG2E, jnp.float32)
        o_ref[h, 0] = jnp.where(i >= j, _lookup(rel_ref, b0, h), NEG_INF)
        o_ref[h, 1] = _lookup(rel_ref, b1, h)
        o_ref[h, BIAS_TILE_FAR] = far
        o_ref[h, BIAS_TILE_EDGE] = jnp.where(j > i, far, NEG_INF)
        o_ref[h, BIAS_TILE_MASKED] = jnp.full((TQ, TQ), NEG_INF, jnp.float32)


def _nsa_bias_tables(rel_bias, seq):
    smem = pl.BlockSpec(memory_space=pltpu.SMEM)
    nq = seq // TQ
    bias_c = pl.pallas_call(
        _bias_cmp_kernel,
        grid=(nq,),
        in_specs=[smem],
        out_specs=pl.BlockSpec((NSA_HEADS, None, N_CMP_PAD, TQ), lambda i: (0, i, 0, 0)),
        out_shape=jax.ShapeDtypeStruct((NSA_HEADS, nq, N_CMP_PAD, TQ), jnp.float32),
        compiler_params=_cparams("parallel"),
        name="nsa_bias_cmp",
    )(rel_bias)
    bias_t = pl.pallas_call(
        _bias_tile_kernel,
        in_specs=[smem],
        out_specs=pl.BlockSpec((NSA_HEADS, N_BIAS_TILES, TQ, TQ), lambda: (0, 0, 0, 0)),
        out_shape=jax.ShapeDtypeStruct((NSA_HEADS, N_BIAS_TILES, TQ, TQ), jnp.float32),
        compiler_params=pltpu.CompilerParams(vmem_limit_bytes=VMEM_LIMIT),
        name="nsa_bias_tiles",
    )(rel_bias)
    return bias_c, bias_t


def _nsa_compress_kernel(transpose_out, x_ref, pe_ref, w_ref, o_ref):
    half = CMP_STRIDE * HEAD_DIM
    x = x_ref[...]
    pe = pe_ref[...]
    top = _dot(_bf16(x + pe[:, :half]), w_ref[:half, :])
    bot = _dot(_bf16(x + pe[:, half:]), w_ref[half:, :])
    n = x.shape[0]
    out = top + pltpu.roll(bot, n - 1, 0)
    o_ref[...] = _bf16(out.T if transpose_out else out)


def _nsa_compress(zb, col, pe, w, batch, seq, transpose_out):
    nchunk = seq // CMP_STRIDE
    half = CMP_STRIDE * HEAD_DIM
    x = zb[:, col:col + NSA_KV_WIDTH].reshape(batch, nchunk, CMP_STRIDE, NSA_KV_HEADS, HEAD_DIM)
    x = x.transpose(0, 3, 1, 2, 4).reshape(batch, NSA_KV_HEADS, nchunk, half)
    return pl.pallas_call(
        functools.partial(_nsa_compress_kernel, transpose_out),
        grid=(batch, NSA_KV_HEADS),
        in_specs=[pl.BlockSpec((None, None, nchunk, half), lambda b, g: (b, g, 0, 0)),
                  pl.BlockSpec((1, 2 * half), lambda b, g: (0, 0)),
                  pl.BlockSpec((2 * half, HEAD_DIM), lambda b, g: (0, 0))],
        out_specs=pl.BlockSpec((None, None, nchunk, HEAD_DIM), lambda b, g: (b, g, 0, 0)),
        out_shape=jax.ShapeDtypeStruct((batch, NSA_KV_HEADS, nchunk, HEAD_DIM), jnp.bfloat16),
        compiler_params=_cparams("parallel", "parallel"),
        name="nsa_compress",
    )(x, pe.reshape(1, 2 * half), _bf16(w.reshape(2 * half, HEAD_DIM)))


def _nsa_attn_kernel(q_ref, kc_ref, vct_ref, ks_ref, vst_ref, kw_ref, vwt_ref, zg_ref, bc_ref, bt_ref,
                     ovl_ref, exp_ref, o_ref, qt_ref, acc_ref, ocmp_ref, oslc_ref):
    qi = pl.program_id(2)
    r4 = NSA_REP
    q = q_ref[...].astype(jnp.float32)
    for r in range(r4):
        qt_ref[:, r * TQ:(r + 1) * TQ] = _bf16(q[:, r * HEAD_DIM:(r + 1) * HEAD_DIM].T)
    qt = qt_ref[...]

    st = _dot(kc_ref[...], qt)
    psum = jnp.zeros((N_CMP_PAD, TQ), jnp.float32)
    ps = []
    for r in range(r4):
        bias = bc_ref[r]
        valid = bias > 0.5 * NEG_INF
        s = jnp.where(valid, st[:, r * TQ:(r + 1) * TQ] * QK_SCALE + bias, NEG_INF)
        e = jnp.exp2(s - jnp.max(s, axis=0, keepdims=True))
        p = jnp.where(valid, e * (1.0 / jnp.sum(e, axis=0, keepdims=True)), 0.0)
        psum = psum + p
        ps.append(_bf16(p))
    ocmp_ref[...] = _dot(vct_ref[...], jnp.concatenate(ps, axis=1))

    imp = _dot(ovl_ref[...], _bf16(psum))
    jb = lax.broadcasted_iota(jnp.int32, (N_SLC, TQ), 0)
    tq = qi * TQ + lax.broadcasted_iota(jnp.int32, (N_SLC, TQ), 1)
    cur = tq // SLC_BLOCK
    forced = ((jb == 0) | (jb == cur) | (jb == cur - 1)).astype(jnp.float32)
    score = jnp.where(jb * SLC_BLOCK <= tq, imp + FORCE_BONUS * forced, NEG_INF)
    rank = jnp.zeros((N_SLC, TQ), jnp.float32)
    for i in range(N_SLC):
        si = score[i:i + 1]
        beats = (si > score) | ((si == score) & (jb > i))
        rank = rank + beats.astype(jnp.float32)
    sel = _bf16((rank < SLC_TOPN).astype(jnp.float32))

    def logits(k_ref, first_tile, n_tiles, tile_of_dist):
        off = pl.multiple_of(first_tile * TQ, TQ)
        st = _dot(k_ref[pl.ds(off, n_tiles * TQ), :], qt)
        tiles = [tile_of_dist(qi - (first_tile + u)) for u in range(n_tiles)]
        out = []
        for r in range(r4):
            bias = jnp.concatenate([bt_ref[r, tl] for tl in tiles], axis=0)
            out.append(st[:, r * TQ:(r + 1) * TQ] * QK_SCALE + bias)
        return off, out

    slc_tile = lambda d: jnp.where(d < 0, BIAS_TILE_MASKED, jnp.minimum(d, BIAS_TILE_FAR))
    keys = SLC_GROUP * TQ
    acc_ref[...] = jnp.zeros_like(acc_ref)

    def slc_body(it, carry):
        m, l = carry
        off, chunks = logits(ks_ref, it * SLC_GROUP, SLC_GROUP, slc_tile)
        selb = (_dot(exp_ref[pl.ds(off, keys), :], sel) - 1.0) * (-NEG_INF)
        p, alpha, m, l = _softmax_step([s + selb for s in chunks], m, l, keys)
        acc_ref[...] = acc_ref[...] * alpha + _dot(vst_ref[:, pl.ds(off, keys)], p)
        return m, l

    init = (jnp.full((1, r4 * TQ), -jnp.inf, jnp.float32), jnp.zeros((1, r4 * TQ), jnp.float32))
    _, l = lax.fori_loop(0, qi // SLC_GROUP + 1, slc_body, init)
    oslc_ref[...] = acc_ref[...] * (1.0 / l)

    far_d = WINDOW // TQ
    win_tile = lambda d: jnp.where(d < 0, BIAS_TILE_MASKED, jnp.where(
        d < BIAS_TILE_FAR, d, jnp.where(d < far_d, BIAS_TILE_FAR, BIAS_TILE_EDGE)))
    off, chunks = logits(kw_ref, jnp.maximum(qi - far_d, 0), WIN_TILES, win_tile)
    p, _, _, l = _softmax_step(chunks, init[0], init[1], WIN_TILES * TQ)
    acc_ref[...] = _dot(vwt_ref[:, pl.ds(off, WIN_TILES * TQ)], p) * (1.0 / l)

    gt = _sigmoid(zg_ref[...]).T
    for r in range(r4):
        lanes = slice(r * TQ, (r + 1) * TQ)
        o = (gt[3 * r:3 * r + 1] * ocmp_ref[:, lanes] + gt[3 * r + 1:3 * r + 2] * oslc_ref[:, lanes]
             + gt[3 * r + 2:3 * r + 3] * acc_ref[:, lanes])
        o_ref[:, r * HEAD_DIM:(r + 1) * HEAD_DIM] = o.T


def _nsa_consts(seq):
    nc = np.arange(N_CMP_PAD)
    blk_start = nc * CMP_STRIDE
    blk_end = blk_start + CMP_BLOCK - 1
    sel_start = np.arange(N_SLC) * SLC_BLOCK
    ovl = ((blk_start[None, :] <= sel_start[:, None] + SLC_BLOCK - 1) & (blk_end[None, :] >= sel_start[:, None])
           & (nc[None, :] < N_CMP_PAD - 1))
    expand = (np.arange(seq)[:, None] // SLC_BLOCK) == np.arange(N_SLC)[None, :]
    return jnp.asarray(ovl, jnp.bfloat16), jnp.asarray(expand, jnp.bfloat16)


def _nsa_attn(za, z_gate, kc, vct, bias_c, bias_t, batch, seq):
    nq = seq // TQ
    ovl, expand = _nsa_consts(seq)
    vst = _heads_t(za, COL_VCS, NSA_KV_HEADS, batch, seq)
    vwt = _heads_t(za, COL_VCW, NSA_KV_HEADS, batch, seq)
    cb = lambda c: c // HEAD_DIM
    keys = lambda col: pl.BlockSpec((seq, HEAD_DIM), lambda b, g, i: (b, cb(col) + g))
    vals = pl.BlockSpec((None, None, HEAD_DIM, seq), lambda b, g, i: (b, g, 0, 0))
    cmp = pl.BlockSpec((None, None, N_CMP_PAD, HEAD_DIM), lambda b, g, i: (b, g, 0, 0))
    qw = NSA_REP * HEAD_DIM
    wide = pltpu.VMEM((HEAD_DIM, NSA_REP * TQ), jnp.float32)
    return pl.pallas_call(
        _nsa_attn_kernel,
        grid=(batch, NSA_KV_HEADS, nq),
        in_specs=[
            pl.BlockSpec((TQ, qw), lambda b, g, i: (b * nq + i, COL_QC // qw + g)),
            cmp, cmp, keys(COL_KCS), vals, keys(COL_KCW), vals,
            pl.BlockSpec((TQ, LANES), lambda b, g, i: (b * nq + i, COL_GATE // LANES + 1 + g)),
            pl.BlockSpec((NSA_REP, None, N_CMP_PAD, TQ), lambda b, g, i: (g, i, 0, 0)),
            pl.BlockSpec((NSA_REP, N_BIAS_TILES, TQ, TQ), lambda b, g, i: (g, 0, 0, 0)),
            pl.BlockSpec((N_SLC, N_CMP_PAD), lambda b, g, i: (0, 0)),
            pl.BlockSpec((seq, N_SLC), lambda b, g, i: (0, 0)),
        ],
        out_specs=pl.BlockSpec((TQ, qw), lambda b, g, i: (b * nq + i, g)),
        out_shape=jax.ShapeDtypeStruct((batch * seq, NSA_WIDTH), jnp.float32),
        scratch_shapes=[pltpu.VMEM((HEAD_DIM, NSA_REP * TQ), jnp.bfloat16), wide, wide, wide],
        compiler_params=_cparams("parallel", "parallel", "arbitrary"),
        name="nsa_attention",
    )(za, kc, vct, za, vst, za, vwt, z_gate, bias_c, bias_t, ovl, expand)


def _mixnorm_kernel(oa_ref, ob_ref, oc_ref, ga_ref, gc_ref, o_ref):
    oa = oa_ref[...]
    oc = oc_ref[...]
    o_ref[:, :FOX_WIDTH] = _bf16(oa * _rms_scale(oa) * ga_ref[...])
    o_ref[:, FOX_WIDTH:FOX_WIDTH + S5_WIDTH] = ob_ref[...]
    o_ref[:, FOX_WIDTH + S5_WIDTH:] = _bf16(oc * _rms_scale(oc) * gc_ref[...])


def _mixnorm(o_a, o_b, o_c, g_a, g_c, tm=ROW_TILE):
    t = o_a.shape[0]
    row = lambda w: pl.BlockSpec((tm, w), lambda i: (i, 0))
    vec = lambda w: pl.BlockSpec((1, w), lambda i: (0, 0))
    return pl.pallas_call(
        _mixnorm_kernel,
        grid=(t // tm,),
        in_specs=[row(FOX_WIDTH), row(S5_WIDTH), row(NSA_WIDTH), vec(FOX_WIDTH), vec(NSA_WIDTH)],
        out_specs=row(D_MODEL),
        out_shape=jax.ShapeDtypeStruct((t, D_MODEL), jnp.bfloat16),
        compiler_params=_cparams("parallel"),
        name="mixer_norm_concat",
    )(o_a, o_b, o_c, g_a.reshape(1, -1), g_c.reshape(1, -1))


def _split_w_in(w):
    f, kv = FOX_WIDTH, NSA_KV_WIDTH
    o = 0
    qa, o = w[:, o:o + f], o + f
    ka, o = w[:, o:o + f], o + f
    va, o = w[:, o:o + f], o + f
    zf, o = w[:, o:o + FOX_HEADS], o + FOX_HEADS
    ub, o = w[:, o:o + S5_WIDTH], o + S5_WIDTH
    qc, o = w[:, o:o + NSA_WIDTH], o + NSA_WIDTH
    kcc_vcc, o = w[:, o:o + 2 * kv], o + 2 * kv
    rest, o = w[:, o:o + 4 * kv], o + 4 * kv
    zg = w[:, o:o + 3 * NSA_HEADS]
    w_attn = _bf16(jnp.concatenate([qa, ka, va, qc, rest], axis=1))
    per_g = 3 * NSA_REP
    gates = [jnp.pad(zf, ((0, 0), (0, LANES - FOX_HEADS)))]
    for g in range(NSA_KV_HEADS):
        gates.append(jnp.pad(zg[:, g * per_g:(g + 1) * per_g], ((0, 0), (0, LANES - per_g))))
    w_f32 = _bf16(jnp.concatenate([ub, kcc_vcc] + gates, axis=1))
    return w_attn, w_f32


def _block_diag(blocks):
    j, n, a, b = blocks.shape
    eye = jnp.eye(n, dtype=blocks.dtype)
    return (blocks[:, :, :, None, :] * eye[None, :, None, :, None]).reshape(j, n * a, n * b)


def _s5_weights(bb_re, bb_im, c_re, c_im):
    n = S5_SLAB // S5_GROUP_CH
    layers = bb_re.shape[0]

    def slabs(m, a, b):
        d = _block_diag(m.reshape(layers * N_S5_SLABS, n, a, b))
        return _bf16(d.reshape(layers, N_S5_SLABS, n * a, n * b))

    inp = lambda m: slabs(m.transpose(0, 1, 3, 2), S5_GROUP_CH, S5_STATE)
    out = lambda m: slabs(m.transpose(0, 1, 3, 2), S5_STATE, S5_GROUP_CH)
    return inp(bb_re), inp(bb_im), out(c_re), out(c_im)


def kernel(x, w_in, b_forget, s5_a_re, s5_a_im, s5_log_dt, s5_b_re, s5_b_im, s5_c_re, s5_c_im, s5_d, s5_w_glu,
           cmp_pe_k, cmp_pe_v, cmp_w_k, cmp_w_v, rel_bias, g_out_fox, g_out_s5, g_out_nsa, w_out, g_pre_mix,
           g_post_mix, g_pre_ffn, g_post_ffn, w_up, conv_w, conv_b, w_down):
    batch, seq, d = x.shape
    t = batch * seq
    xf = x.reshape(t, d)

    bias_c, bias_t = _nsa_bias_tables(rel_bias, seq)
    bb_re, bb_im, pw_re, pw_im = _s5_discretize(s5_a_re, s5_a_im, s5_log_dt, s5_b_re, s5_b_im)
    s5w = _s5_weights(bb_re, bb_im, s5_c_re, s5_c_im)

    hn = _rmsnorm_cast(xf, g_pre_mix[0])
    for l in range(DEPTH):
        w_attn, w_f32 = _split_w_in(w_in[l])
        za = _matmul(hn, w_attn, IN_PROJ_TN, jnp.bfloat16, "in_proj_attn")
        zb = _matmul(hn, w_f32, N_F32 // 3, jnp.float32, "in_proj_f32")

        ccol, crow = _fox_gate(zb, b_forget[l], batch, seq)
        o_a = _fox_attn(za, ccol, crow, batch, seq)

        y_b = _s5_scan(zb, *(w[l] for w in s5w), s5_d[l], pw_re[l], pw_im[l], batch, seq)
        o_b = _s5_glu(y_b, _bf16(s5_w_glu[l]), g_out_s5[l])

        kc = _nsa_compress(zb, COL_KCC, cmp_pe_k[l], cmp_w_k[l], batch, seq, False)
        vct = _nsa_compress(zb, COL_VCC, cmp_pe_v[l], cmp_w_v[l], batch, seq, True)
        o_c = _nsa_attn(za, zb, kc, vct, bias_c, bias_t, batch, seq)

        mixed = _mixnorm(o_a, o_b, o_c, g_out_fox[l], g_out_nsa[l])
        xf, hn = _matmul_resid_cols(mixed, _bf16(w_out[l]), xf, g_post_mix[l], g_pre_ffn[l], "out_proj")

        wg, wv = _bf16(w_up[l][:, :D_FF]), _bf16(w_up[l][:, D_FF:])
        cwg, cwv = conv_w[l][:, :D_FF], conv_w[l][:, D_FF:]
        cbg, cbv = conv_b[l][None, :D_FF], conv_b[l][None, D_FF:]
        act = _ffn_up(hn, wg, wv, cwg, cwv, cbg, cbv, seq)
        g_next = g_pre_mix[l + 1] if l + 1 < DEPTH else jnp.ones((d,), jnp.float32)
        xf, hn = _matmul_resid(act, _bf16(w_down[l]), xf, g_post_ffn[l], g_next, "ffn_down")
    return xf.reshape(batch, seq, d)
```

```python
import functools
import math

import numpy as np
import jax
import jax.numpy as jnp
from jax import lax
from jax.experimental import pallas as pl
from jax.experimental.pallas import tpu as pltpu

D_MODEL = 4096
DEPTH = 4
HEAD_DIM = 128
FOX_WIDTH = 1024
FOX_HEADS = 8
S5_WIDTH = 1024
S5_GROUP_CH = 16
S5_GROUPS = 64
S5_STATE = 64
NSA_WIDTH = 2048
NSA_HEADS = 16
NSA_KV_HEADS = 4
NSA_REP = 4
NSA_KV_WIDTH = 512
CMP_BLOCK = 32
CMP_STRIDE = 16
SLC_BLOCK = 64
SLC_TOPN = 16
WINDOW = 512
FORCE_BONUS = 1e3
REL_BUCKETS = 32
REL_EXACT = 16
REL_MAX_DIST = 128
D_FF = 11008
RMS_EPS = 1e-6
NEG_INF = -1e30
LOG2E = 1.4426950408889634
QK_SCALE = HEAD_DIM ** -0.5 * LOG2E

LANES = 128
SUBLANES = 8
VMEM_LIMIT = 60 * 1024 * 1024

S5_SLAB = 256
S5_SLAB_STATE = S5_SLAB // S5_GROUP_CH * S5_STATE
N_S5_SLABS = S5_WIDTH // S5_SLAB
COL_QA, COL_KA, COL_QC, COL_KCS, COL_KCW, COL_V = 0, 1024, 2048, 4096, 4608, 5120
N_ATTN = 7168
V_HEAD_FOX, V_HEAD_SLC, V_HEAD_WIN, N_V_HEADS = 0, FOX_HEADS, FOX_HEADS + NSA_KV_HEADS, FOX_HEADS + 2 * NSA_KV_HEADS
COL_UB, COL_KCC, COL_VCC, COL_GATE = 0, 1024, 1536, 2048
N_GATE = LANES * (1 + NSA_KV_HEADS)
N_F32 = COL_GATE + N_GATE
TQ = 128
SLC_GROUP = 4
WIN_TILES = WINDOW // TQ + 1
BIAS_TILE_FAR, BIAS_TILE_EDGE, BIAS_TILE_MASKED, N_BIAS_TILES = 2, 3, 4, 5
FOX_TQ = 512
N_CMP_PAD = 128
N_SLC = 32
ROW_TILE = 512
IN_PROJ_TN = 1024
RESID_TK = 1024
RESID_COLS = 512
RESID_ROWS = 128
FFN_TN = 512
FFN_CHUNK = 256


def _cparams(*sem):
    return pltpu.CompilerParams(dimension_semantics=sem, vmem_limit_bytes=VMEM_LIMIT)


def _bf16(x):
    return x.astype(jnp.bfloat16)


def _dot(a, b):
    return jnp.dot(a, b, preferred_element_type=jnp.float32)


def _rms_scale(x):
    return lax.rsqrt(jnp.mean(x * x, axis=-1, keepdims=True) + RMS_EPS)


def _gelu_tanh(x):
    return 0.5 * x * (1.0 + jnp.tanh(math.sqrt(2.0 / math.pi) * (x + 0.044715 * (x * x * x))))


def _sigmoid(x):
    return 1.0 / (1.0 + jnp.exp(-x))


def _rmsnorm_cast_kernel(x_ref, g_ref, o_ref):
    x = x_ref[...]
    o_ref[...] = _bf16(x * _rms_scale(x) * g_ref[...])


def _rmsnorm_cast(x, g, tm=ROW_TILE):
    t, d = x.shape
    return pl.pallas_call(
        _rmsnorm_cast_kernel,
        grid=(t // tm,),
        in_specs=[pl.BlockSpec((tm, d), lambda i: (i, 0)), pl.BlockSpec((1, d), lambda i: (0, 0))],
        out_specs=pl.BlockSpec((tm, d), lambda i: (i, 0)),
        out_shape=jax.ShapeDtypeStruct((t, d), jnp.bfloat16),
        compiler_params=_cparams("parallel"),
        name="rmsnorm_cast",
    )(x, g.reshape(1, d))


def _matmul_kernel(a_ref, w_ref, o_ref):
    o_ref[...] = _dot(a_ref[...], w_ref[...]).astype(o_ref.dtype)


def _matmul(a, w, tn, out_dtype, name, tm=ROW_TILE):
    t, k = a.shape
    n = w.shape[1]
    return pl.pallas_call(
        _matmul_kernel,
        grid=(t // tm, n // tn),
        in_specs=[pl.BlockSpec((tm, k), lambda i, j: (i, 0)), pl.BlockSpec((k, tn), lambda i, j: (0, j))],
        out_specs=pl.BlockSpec((tm, tn), lambda i, j: (i, j)),
        out_shape=jax.ShapeDtypeStruct((t, n), out_dtype),
        compiler_params=_cparams("parallel", "arbitrary"),
        name=name,
    )(a, w)


def _resid_epilogue(x_ref, gpost_ref, gnext_ref, xo_ref, hn_ref):
    tm = xo_ref.shape[0]
    for r in range(tm // RESID_ROWS):
        rows = slice(r * RESID_ROWS, (r + 1) * RESID_ROWS)
        f = xo_ref[rows, :]
        xn = x_ref[rows, :] + f * _rms_scale(f) * gpost_ref[...]
        xo_ref[rows, :] = xn
        hn_ref[rows, :] = _bf16(xn * _rms_scale(xn) * gnext_ref[...])


def _residual_copy(x_hbm, xbuf_ref, sem):
    tm = xbuf_ref.shape[0]
    rows = pl.ds(pl.multiple_of(pl.program_id(0) * tm, tm), tm)
    return pltpu.make_async_copy(x_hbm.at[rows, :], xbuf_ref, sem)


def _matmul_resid_cols_kernel(a_ref, w_ref, x_hbm, gpost_ref, gnext_ref, xo_ref, hn_ref, xbuf_ref, sem):
    j = pl.program_id(1)
    tn = w_ref.shape[1]
    pl.when(j == 0)(lambda: _residual_copy(x_hbm, xbuf_ref, sem).start())
    xo_ref[:, pl.ds(pl.multiple_of(j * tn, tn), tn)] = _dot(a_ref[...], w_ref[...])

    @pl.when(j == pl.num_programs(1) - 1)
    def _():
        _residual_copy(x_hbm, xbuf_ref, sem).wait()
        _resid_epilogue(xbuf_ref, gpost_ref, gnext_ref, xo_ref, hn_ref)


def _matmul_resid_cols(a, w, x, g_post, g_next, name, tm=ROW_TILE, tn=RESID_COLS):
    t, kk = a.shape
    d = w.shape[1]
    return pl.pallas_call(
        _matmul_resid_cols_kernel,
        grid=(t // tm, d // tn),
        in_specs=[
            pl.BlockSpec((tm, kk), lambda i, j: (i, 0)),
            pl.BlockSpec((kk, tn), lambda i, j: (0, j)),
            pl.BlockSpec(memory_space=pl.ANY),
            pl.BlockSpec((1, d), lambda i, j: (0, 0)),
            pl.BlockSpec((1, d), lambda i, j: (0, 0)),
        ],
        out_specs=[pl.BlockSpec((tm, d), lambda i, j: (i, 0)), pl.BlockSpec((tm, d), lambda i, j: (i, 0))],
        out_shape=[jax.ShapeDtypeStruct((t, d), jnp.float32), jax.ShapeDtypeStruct((t, d), jnp.bfloat16)],
        scratch_shapes=[pltpu.VMEM((tm, d), jnp.float32), pltpu.SemaphoreType.DMA(())],
        compiler_params=_cparams("parallel", "arbitrary"),
        name=name,
    )(a, w, x, g_post.reshape(1, d), g_next.reshape(1, d))


def _matmul_resid_kernel(k_total, a_ref, w_ref, x_hbm, gpost_ref, gnext_ref, xo_ref, hn_ref, xbuf_ref, sem):
    k = pl.program_id(1)
    last = pl.num_programs(1) - 1
    tm, d = xo_ref.shape
    tk = a_ref.shape[1]
    tail = k_total % tk

    @pl.when(k == 0)
    def _():
        _residual_copy(x_hbm, xbuf_ref, sem).start()
        xo_ref[...] = jnp.zeros_like(xo_ref)

    def accumulate(masked):
        a = a_ref[...]
        if masked:
            acol = lax.broadcasted_iota(jnp.int32, a.shape, 1) < tail
            a = _bf16(jnp.where(acol, a.astype(jnp.float32), 0.0))
            wrow = lax.broadcasted_iota(jnp.int32, (tk, RESID_COLS), 0) < tail
        for n in range(d // RESID_COLS):
            cols = slice(n * RESID_COLS, (n + 1) * RESID_COLS)
            w = w_ref[:, cols]
            if masked:
                w = _bf16(jnp.where(wrow, w.astype(jnp.float32), 0.0))
            xo_ref[:, cols] += _dot(a, w)

    if tail == 0:
        accumulate(False)
    else:
        pl.when(k < last)(lambda: accumulate(False))
        pl.when(k == last)(lambda: accumulate(True))

    @pl.when(k == last)
    def _():
        _residual_copy(x_hbm, xbuf_ref, sem).wait()
        _resid_epilogue(xbuf_ref, gpost_ref, gnext_ref, xo_ref, hn_ref)


def _matmul_resid(a, w, x, g_post, g_next, name, tm=ROW_TILE, tk=RESID_TK):
    t, kk = a.shape
    d = w.shape[1]
    return pl.pallas_call(
        functools.partial(_matmul_resid_kernel, kk),
        grid=(t // tm, pl.cdiv(kk, tk)),
        in_specs=[
            pl.BlockSpec((tm, tk), lambda i, k: (i, k)),
            pl.BlockSpec((tk, d), lambda i, k: (k, 0)),
            pl.BlockSpec(memory_space=pl.ANY),
            pl.BlockSpec((1, d), lambda i, k: (0, 0)),
            pl.BlockSpec((1, d), lambda i, k: (0, 0)),
        ],
        out_specs=[pl.BlockSpec((tm, d), lambda i, k: (i, 0)), pl.BlockSpec((tm, d), lambda i, k: (i, 0))],
        out_shape=[jax.ShapeDtypeStruct((t, d), jnp.float32), jax.ShapeDtypeStruct((t, d), jnp.bfloat16)],
        scratch_shapes=[pltpu.VMEM((tm, d), jnp.float32), pltpu.SemaphoreType.DMA(())],
        compiler_params=_cparams("parallel", "arbitrary"),
        name=name,
    )(a, w, x, g_post.reshape(1, d), g_next.reshape(1, d))


def _ffn_up_kernel(seq, h_ref, halo_ref, wg_ref, wv_ref, cwg_ref, cwv_ref, cbg_ref, cbv_ref, o_ref):
    i = pl.program_id(0)
    tm, tn = o_ref.shape
    h = h_ref[...]
    keep = ((i * tm) % seq != 0).astype(jnp.float32)
    halo = halo_ref[...]
    row = lax.broadcasted_iota(jnp.int32, (tm, 1), 0)

    def conv(w_ref, cw_ref, cb_ref, cols):
        w = w_ref[:, cols]
        u = _dot(h, w)
        uh = _dot(halo, w) * keep
        p1 = jnp.where(row == 0, uh[7:8], pltpu.roll(u, 1, 0))
        p2 = jnp.where(row == 0, uh[6:7], jnp.where(row == 1, uh[7:8], pltpu.roll(u, 2, 0)))
        cw = cw_ref[:, cols]
        return cb_ref[:, cols] + cw[0:1] * p2 + cw[1:2] * p1 + cw[2:3] * u

    for c in range(tn // FFN_CHUNK):
        cols = slice(c * FFN_CHUNK, (c + 1) * FFN_CHUNK)
        gate = conv(wg_ref, cwg_ref, cbg_ref, cols)
        val = conv(wv_ref, cwv_ref, cbv_ref, cols)
        o_ref[:, cols] = _bf16(_gelu_tanh(gate) * val)


def _ffn_up(hn, wg, wv, cwg, cwv, cbg, cbv, seq, tm=ROW_TILE, tn=FFN_TN):
    t, d = hn.shape
    f = wg.shape[1]
    hb = tm // SUBLANES
    return pl.pallas_call(
        functools.partial(_ffn_up_kernel, seq),
        grid=(t // tm, pl.cdiv(f, tn)),
        in_specs=[
            pl.BlockSpec((tm, d), lambda i, j: (i, 0)),
            pl.BlockSpec((SUBLANES, d), lambda i, j: (jnp.maximum(i * hb - 1, 0), 0)),
            pl.BlockSpec((d, tn), lambda i, j: (0, j)),
            pl.BlockSpec((d, tn), lambda i, j: (0, j)),
            pl.BlockSpec((3, tn), lambda i, j: (0, j)),
            pl.BlockSpec((3, tn), lambda i, j: (0, j)),
            pl.BlockSpec((1, tn), lambda i, j: (0, j)),
            pl.BlockSpec((1, tn), lambda i, j: (0, j)),
        ],
        out_specs=pl.BlockSpec((tm, tn), lambda i, j: (i, j)),
        out_shape=jax.ShapeDtypeStruct((t, f), jnp.bfloat16),
        compiler_params=_cparams("parallel", "arbitrary"),
        name="ffn_up_conv_geglu",
    )(hn, hn, wg, wv, cwg, cwv, cbg, cbv)


def _softmax_step(s_chunks, m, l, keys):
    ps, alphas, ms, ls = [], [], [], []
    for c, s in enumerate(s_chunks):
        lanes = slice(c * LANES, (c + 1) * LANES)
        m_old = m[:, lanes]
        m_new = jnp.maximum(m_old, jnp.max(s, axis=0, keepdims=True))
        alpha = jnp.exp2(m_old - m_new)
        p = jnp.exp2(s - m_new)
        ls.append(alpha * l[:, lanes] + jnp.sum(p, axis=0, keepdims=True))
        ms.append(m_new)
        alphas.append(alpha)
        p = _bf16(p)
        if s.shape[0] < keys:
            p = jnp.concatenate([p, jnp.zeros((keys - s.shape[0], LANES), jnp.bfloat16)], axis=0)
        ps.append(p)
    cat = lambda xs: jnp.concatenate(xs, axis=1)
    return cat(ps), cat(alphas), cat(ms), cat(ls)


def _split3(x):
    hi = _bf16(x)
    r = x - hi.astype(jnp.float32)
    mid = _bf16(r)
    lo = _bf16(r - mid.astype(jnp.float32))
    return hi, mid, lo


def _fox_gate_kernel(zf_ref, bf_ref, ccol_ref, crow_ref):
    seq = zf_ref.shape[0]
    blk = 256
    x = zf_ref[...] + bf_ref[...]
    logf = -(jnp.maximum(-x, 0.0) + jnp.log(1.0 + jnp.exp(-jnp.abs(x))))
    r = lax.broadcasted_iota(jnp.int32, (blk, blk), 0)
    c = lax.broadcasted_iota(jnp.int32, (blk, blk), 1)
    tri = _bf16((c <= r).astype(jnp.float32))
    carry = jnp.zeros((1, LANES), jnp.float32)
    for b in range(seq // blk):
        hi, mid, lo = _split3(logf[b * blk:(b + 1) * blk])
        cs = _dot(tri, hi) + _dot(tri, mid) + _dot(tri, lo) + carry
        ccol_ref[b * blk:(b + 1) * blk, :] = cs
        carry = cs[blk - 1:blk]
    crow_ref[...] = ccol_ref[...].T


def _fox_gate(zb, b_f, batch, seq):
    bf = jnp.zeros((1, LANES), jnp.float32).at[0, :FOX_HEADS].set(b_f)
    return pl.pallas_call(
        _fox_gate_kernel,
        grid=(batch,),
        in_specs=[pl.BlockSpec((seq, LANES), lambda b: (b, COL_GATE // LANES)),
                  pl.BlockSpec((1, LANES), lambda b: (0, 0))],
        out_specs=[pl.BlockSpec((seq, LANES), lambda b: (b, 0)), pl.BlockSpec((None, LANES, seq), lambda b: (b, 0, 0))],
        out_shape=[jax.ShapeDtypeStruct((batch * seq, LANES), jnp.float32),
                   jax.ShapeDtypeStruct((batch, LANES, seq), jnp.float32)],
        compiler_params=_cparams("parallel"),
        name="fox_gate_cumsum",
    )(zb, bf)


def _fox_attn_kernel(q_ref, k_ref, vt_ref, cq_ref, ck_ref, o_ref, acc_ref):
    qi = pl.program_id(2)
    tq = q_ref.shape[0]
    nch = tq // LANES
    qt = _bf16(q_ref[...].astype(jnp.float32).T)
    cq = cq_ref[...] * LOG2E
    acc_ref[...] = jnp.zeros_like(acc_ref)

    def step(kt, m, l, diagonal):
        off = pl.multiple_of(kt * tq, tq)
        st = _dot(k_ref[pl.ds(off, tq), :], qt)
        ck = ck_ref[pl.ds(off, tq), :] * LOG2E
        chunks = []
        for c in range(nch):
            lanes = slice(c * LANES, (c + 1) * LANES)
            if diagonal:
                rows = (c + 1) * LANES
                s = st[:rows, lanes] * QK_SCALE + (cq[:, lanes] - ck[:rows])
                krow = lax.broadcasted_iota(jnp.int32, (rows, LANES), 0)
                qcol = c * LANES + lax.broadcasted_iota(jnp.int32, (rows, LANES), 1)
                s = jnp.where(krow <= qcol, s, NEG_INF)
            else:
                s = st[:, lanes] * QK_SCALE + (cq[:, lanes] - ck)
            chunks.append(s)
        p, alpha, m, l = _softmax_step(chunks, m, l, tq)
        acc_ref[...] = acc_ref[...] * alpha + _dot(vt_ref[:, pl.ds(off, tq)], p)
        return m, l

    init = (jnp.full((1, tq), -jnp.inf, jnp.float32), jnp.zeros((1, tq), jnp.float32))
    m, l = lax.fori_loop(0, qi, lambda kt, c: step(kt, c[0], c[1], False), init)
    _, l = step(qi, m, l, True)
    o_ref[...] = (acc_ref[...] * (1.0 / l)).T


def _values_t(za, batch, seq):
    v = za[:, COL_V:COL_V + N_V_HEADS * HEAD_DIM]
    return v.reshape(batch, seq, N_V_HEADS, HEAD_DIM).transpose(0, 2, 3, 1)


def _fox_attn(za, vt, ccol, crow, batch, seq, tq=FOX_TQ):
    nq = seq // tq
    ck = ccol[:, :FOX_HEADS].reshape(batch, seq, FOX_HEADS).transpose(0, 2, 1).reshape(batch, FOX_HEADS, seq, 1)
    cq = crow[:, :FOX_HEADS, :].reshape(batch, FOX_HEADS, 1, seq)
    cb = lambda c: c // HEAD_DIM
    return pl.pallas_call(
        _fox_attn_kernel,
        grid=(batch, FOX_HEADS, nq),
        in_specs=[
            pl.BlockSpec((tq, HEAD_DIM), lambda b, h, i: (b * nq + i, cb(COL_QA) + h)),
            pl.BlockSpec((seq, HEAD_DIM), lambda b, h, i: (b, cb(COL_KA) + h)),
            pl.BlockSpec((None, None, HEAD_DIM, seq), lambda b, h, i: (b, V_HEAD_FOX + h, 0, 0)),
            pl.BlockSpec((None, None, 1, tq), lambda b, h, i: (b, h, 0, i)),
            pl.BlockSpec((None, None, seq, 1), lambda b, h, i: (b, h, 0, 0)),
        ],
        out_specs=pl.BlockSpec((tq, HEAD_DIM), lambda b, h, i: (b * nq + i, h)),
        out_shape=jax.ShapeDtypeStruct((batch * seq, FOX_WIDTH), jnp.float32),
        scratch_shapes=[pltpu.VMEM((HEAD_DIM, tq), jnp.float32)],
        compiler_params=_cparams("parallel", "parallel", "arbitrary"),
        name="fox_attention",
    )(za, za, vt, cq, ck)


def _s5_disc_kernel(are_ref, aim_ref, ldt_ref, bre_ref, bim_ref, bbre_ref, bbim_ref, pwre_ref, pwim_ref):
    lam_re = jnp.minimum(are_ref[...], -1e-4)
    lam_im = aim_ref[...]
    dt = jnp.exp(ldt_ref[...])
    mag = jnp.exp(lam_re * dt)
    ang = lam_im * dt
    lb_re, lb_im = mag * jnp.cos(ang), mag * jnp.sin(ang)
    den = lam_re * lam_re + lam_im * lam_im
    nr, ni = lb_re - 1.0, lb_im
    coef_re = (nr * lam_re + ni * lam_im) / den
    coef_im = (ni * lam_re - nr * lam_im) / den
    b_re, b_im = bre_ref[...], bim_ref[...]
    bbre_ref[...] = coef_re * b_re - coef_im * b_im
    bbim_ref[...] = coef_re * b_im + coef_im * b_re
    p_re, p_im = lb_re, lb_im
    for i in range(SUBLANES):
        pwre_ref[i:i + 1, :] = p_re
        pwim_ref[i:i + 1, :] = p_im
        p_re, p_im = p_re * lb_re - p_im * lb_im, p_re * lb_im + p_im * lb_re


def _s5_discretize(a_re, a_im, log_dt, b_re, b_im):
    l, g, p = a_re.shape
    h = b_re.shape[-1]
    n = l * g * p
    row = lambda a: a.reshape(1, n)
    ldt = jnp.broadcast_to(log_dt[:, :, None], (l, g, p))
    bt = lambda b: b.reshape(n, h).T
    full = lambda r: pl.BlockSpec((r, n), lambda: (0, 0))
    bbre, bbim, pwre, pwim = pl.pallas_call(
        _s5_disc_kernel,
        in_specs=[full(1), full(1), full(1), full(h), full(h)],
        out_specs=[full(h), full(h), full(SUBLANES), full(SUBLANES)],
        out_shape=[jax.ShapeDtypeStruct((h, n), jnp.float32)] * 2 + [jax.ShapeDtypeStruct((SUBLANES, n), jnp.float32)] * 2,
        compiler_params=pltpu.CompilerParams(vmem_limit_bytes=VMEM_LIMIT),
        name="s5_discretize",
    )(row(a_re), row(a_im), row(ldt), bt(b_re), bt(b_im))
    unb = lambda b: b.T.reshape(l, g, p, h)
    unp = lambda q: q.reshape(SUBLANES, l, g * p).transpose(1, 0, 2)
    return unb(bbre), unb(bbim), unp(pwre), unp(pwim)


def _s5_scan_kernel(u_ref, bre_ref, bim_ref, cre_ref, cim_ref, d_ref, pwre_ref, pwim_ref, y_ref, hre_ref, him_ref):
    seq = u_ref.shape[0]
    u = u_ref[...]
    ub = _bf16(u)
    hre_ref[...] = _dot(ub, bre_ref[...])
    him_ref[...] = _dot(ub, bim_ref[...])

    pw_re, pw_im = pwre_ref[...], pwim_ref[...]
    sub = lax.broadcasted_iota(jnp.int32, pw_re.shape, 0)

    def shifted(i, k):
        m = sub >= k
        return (jnp.where(m, pw_re[i:i + 1], 0.0), jnp.where(m, pw_im[i:i + 1], 0.0))

    steps = [(1,) + shifted(0, 1), (2,) + shifted(1, 2), (4,) + shifted(3, 4)]

    def tile(j, carry):
        c_re, c_im = carry
        off = pl.multiple_of(j * SUBLANES, SUBLANES)
        x_re = hre_ref[pl.ds(off, SUBLANES), :]
        x_im = him_ref[pl.ds(off, SUBLANES), :]
        for k, a_re, a_im in steps:
            r_re, r_im = pltpu.roll(x_re, k, 0), pltpu.roll(x_im, k, 0)
            x_re, x_im = (x_re + a_re * r_re - a_im * r_im, x_im + a_re * r_im + a_im * r_re)
        x_re, x_im = (x_re + pw_re * c_re - pw_im * c_im, x_im + pw_re * c_im + pw_im * c_re)
        hre_ref[pl.ds(off, SUBLANES), :] = x_re
        him_ref[pl.ds(off, SUBLANES), :] = x_im
        return x_re[SUBLANES - 1:SUBLANES], x_im[SUBLANES - 1:SUBLANES]

    zero = jnp.zeros((1, pw_re.shape[1]), jnp.float32)
    lax.fori_loop(0, seq // SUBLANES, tile, (zero, zero))

    y = _dot(_bf16(hre_ref[...]), cre_ref[...]) - _dot(_bf16(him_ref[...]), cim_ref[...]) + d_ref[...] * u
    y_ref[...] = _gelu_tanh(y)


def _s5_scan(zb, b_big_re, b_big_im, c_big_re, c_big_im, d_skip, pw_re, pw_im, batch, seq):
    w = S5_SLAB_STATE
    return pl.pallas_call(
        _s5_scan_kernel,
        grid=(batch, N_S5_SLABS),
        in_specs=[
            pl.BlockSpec((seq, S5_SLAB), lambda b, j: (b, COL_UB // S5_SLAB + j)),
            pl.BlockSpec((None, S5_SLAB, w), lambda b, j: (j, 0, 0)),
            pl.BlockSpec((None, S5_SLAB, w), lambda b, j: (j, 0, 0)),
            pl.BlockSpec((None, w, S5_SLAB), lambda b, j: (j, 0, 0)),
            pl.BlockSpec((None, w, S5_SLAB), lambda b, j: (j, 0, 0)),
            pl.BlockSpec((1, S5_SLAB), lambda b, j: (0, j)),
            pl.BlockSpec((SUBLANES, w), lambda b, j: (0, j)),
            pl.BlockSpec((SUBLANES, w), lambda b, j: (0, j)),
        ],
        out_specs=pl.BlockSpec((seq, S5_SLAB), lambda b, j: (b, j)),
        out_shape=jax.ShapeDtypeStruct((batch * seq, S5_WIDTH), jnp.float32),
        scratch_shapes=[pltpu.VMEM((seq, w), jnp.float32), pltpu.VMEM((seq, w), jnp.float32)],
        compiler_params=_cparams("parallel", "arbitrary"),
        name="s5_scan",
    )(zb, b_big_re, b_big_im, c_big_re, c_big_im, d_skip.reshape(1, S5_WIDTH), pw_re, pw_im)


def _s5_glu_kernel(y_ref, w_ref, g_ref, o_ref):
    y = y_ref[...]
    o = y * _sigmoid(_dot(_bf16(y), w_ref[...]))
    o_ref[...] = _bf16(o * _rms_scale(o) * g_ref[...])


def _s5_glu(y, w_glu, g_out, tm=ROW_TILE):
    t, d = y.shape
    return pl.pallas_call(
        _s5_glu_kernel,
        grid=(t // tm,),
        in_specs=[pl.BlockSpec((tm, d), lambda i: (i, 0)), pl.BlockSpec((d, d), lambda i: (0, 0)),
                  pl.BlockSpec((1, d), lambda i: (0, 0))],
        out_specs=pl.BlockSpec((tm, d), lambda i: (i, 0)),
        out_shape=jax.ShapeDtypeStruct((t, d), jnp.bfloat16),
        compiler_params=_cparams("parallel"),
        name="s5_glu_norm",
    )(y, w_glu, g_out.reshape(1, d))


def _t5_bucket(dist):
    n = jnp.maximum(dist, 0)
    nf = jnp.maximum(n, 1).astype(jnp.float32)
    large = REL_EXACT + (jnp.log(nf / REL_EXACT) / math.log(REL_MAX_DIST / REL_EXACT)
                         * (REL_BUCKETS - REL_EXACT)).astype(jnp.int32)
    return jnp.where(n < REL_EXACT, n, jnp.minimum(large, REL_BUCKETS - 1))


def _lookup(rel_ref, bucket, h):
    out = jnp.zeros(bucket.shape, jnp.float32)
    for b in range(REL_BUCKETS):
        out = jnp.where(bucket == b, rel_ref[b, h] * LOG2E, out)
    return out


def _bias_cmp_kernel(rel_ref, o_ref):
    qi = pl.program_id(0)
    n = lax.broadcasted_iota(jnp.int32, (N_CMP_PAD, TQ), 0)
    t = qi * TQ + lax.broadcasted_iota(jnp.int32, (N_CMP_PAD, TQ), 1)
    dist = t - (n * CMP_STRIDE + CMP_BLOCK - 1)
    valid = (dist >= 0) & (n < N_CMP_PAD - 1)
    bucket = _t5_bucket(dist)
    for h in range(NSA_HEADS):
        o_ref[h] = jnp.where(valid, _lookup(rel_ref, bucket, h), NEG_INF)


def _bias_tile_kernel(rel_ref, o_ref):
    j = lax.broadcasted_iota(jnp.int32, (TQ, TQ), 0)
    i = lax.broadcasted_iota(jnp.int32, (TQ, TQ), 1)
    b0 = _t5_bucket(i - j)
    b1 = _t5_bucket(i - j + TQ)
    for h in range(NSA_HEADS):
        far = jnp.full((TQ, TQ), rel_ref[REL_BUCKETS - 1, h] * LOG2E, jnp.float32)
        o_ref[h, 0] = jnp.where(i >= j, _lookup(rel_ref, b0, h), NEG_INF)
        o_ref[h, 1] = _lookup(rel_ref, b1, h)
        o_ref[h, BIAS_TILE_FAR] = far
        o_ref[h, BIAS_TILE_EDGE] = jnp.where(j > i, far, NEG_INF)
        o_ref[h, BIAS_TILE_MASKED] = jnp.full((TQ, TQ), NEG_INF, jnp.float32)


def _nsa_bias_tables(rel_bias, seq):
    smem = pl.BlockSpec(memory_space=pltpu.SMEM)
    nq = seq // TQ
    bias_c = pl.pallas_call(
        _bias_cmp_kernel,
        grid=(nq,),
        in_specs=[smem],
        out_specs=pl.BlockSpec((NSA_HEADS, None, N_CMP_PAD, TQ), lambda i: (0, i, 0, 0)),
        out_shape=jax.ShapeDtypeStruct((NSA_HEADS, nq, N_CMP_PAD, TQ), jnp.float32),
        compiler_params=_cparams("parallel"),
        name="nsa_bias_cmp",
    )(rel_bias)
    bias_t = pl.pallas_call(
        _bias_tile_kernel,
        in_specs=[smem],
        out_specs=pl.BlockSpec((NSA_HEADS, N_BIAS_TILES, TQ, TQ), lambda: (0, 0, 0, 0)),
        out_shape=jax.ShapeDtypeStruct((NSA_HEADS, N_BIAS_TILES, TQ, TQ), jnp.float32),
        compiler_params=pltpu.CompilerParams(vmem_limit_bytes=VMEM_LIMIT),
        name="nsa_bias_tiles",
    )(rel_bias)
    return bias_c, bias_t


def _nsa_compress_kernel(transpose_out, x_ref, pe_ref, w_ref, o_ref):
    half = CMP_STRIDE * HEAD_DIM
    x = x_ref[...]
    pe = pe_ref[...]
    top = _dot(_bf16(x + pe[:, :half]), w_ref[:half, :])
    bot = _dot(_bf16(x + pe[:, half:]), w_ref[half:, :])
    n = x.shape[0]
    out = top + pltpu.roll(bot, n - 1, 0)
    o_ref[...] = _bf16(out.T if transpose_out else out)


def _compress_chunks(zb, batch, seq):
    nchunk = seq // CMP_STRIDE
    heads = 2 * NSA_KV_HEADS
    x = zb[:, COL_KCC:COL_KCC + heads * HEAD_DIM].reshape(batch, nchunk, CMP_STRIDE, heads, HEAD_DIM)
    return x.transpose(0, 3, 1, 2, 4).reshape(batch, heads, nchunk, CMP_STRIDE * HEAD_DIM)


def _nsa_compress(x, head0, pe, w, transpose_out):
    batch, _, nchunk, half = x.shape
    return pl.pallas_call(
        functools.partial(_nsa_compress_kernel, transpose_out),
        grid=(batch, NSA_KV_HEADS),
        in_specs=[pl.BlockSpec((None, None, nchunk, half), lambda b, g: (b, head0 + g, 0, 0)),
                  pl.BlockSpec((1, 2 * half), lambda b, g: (0, 0)),
                  pl.BlockSpec((2 * half, HEAD_DIM), lambda b, g: (0, 0))],
        out_specs=pl.BlockSpec((None, None, nchunk, HEAD_DIM), lambda b, g: (b, g, 0, 0)),
        out_shape=jax.ShapeDtypeStruct((batch, NSA_KV_HEADS, nchunk, HEAD_DIM), jnp.bfloat16),
        compiler_params=_cparams("parallel", "parallel"),
        name="nsa_compress",
    )(x, pe.reshape(1, 2 * half), _bf16(w.reshape(2 * half, HEAD_DIM)))


def _nsa_attn_kernel(q_ref, kc_ref, vct_ref, ks_ref, vst_ref, kw_ref, vwt_ref, zg_ref, bc_ref, bt_ref,
                     ovl_ref, exp_ref, o_ref, qt_ref, acc_ref, ocmp_ref, oslc_ref):
    qi = pl.program_id(2)
    r4 = NSA_REP
    q = q_ref[...].astype(jnp.float32)
    for r in range(r4):
        qt_ref[:, r * TQ:(r + 1) * TQ] = _bf16(q[:, r * HEAD_DIM:(r + 1) * HEAD_DIM].T)
    qt = qt_ref[...]

    st = _dot(kc_ref[...], qt)
    psum = jnp.zeros((N_CMP_PAD, TQ), jnp.float32)
    ps = []
    for r in range(r4):
        bias = bc_ref[r]
        valid = bias > 0.5 * NEG_INF
        s = jnp.where(valid, st[:, r * TQ:(r + 1) * TQ] * QK_SCALE + bias, NEG_INF)
        e = jnp.exp2(s - jnp.max(s, axis=0, keepdims=True))
        p = jnp.where(valid, e * (1.0 / jnp.sum(e, axis=0, keepdims=True)), 0.0)
        psum = psum + p
        ps.append(_bf16(p))
    ocmp_ref[...] = _dot(vct_ref[...], jnp.concatenate(ps, axis=1))

    imp = _dot(ovl_ref[...], _bf16(psum))
    jb = lax.broadcasted_iota(jnp.int32, (N_SLC, TQ), 0)
    tq = qi * TQ + lax.broadcasted_iota(jnp.int32, (N_SLC, TQ), 1)
    cur = tq // SLC_BLOCK
    forced = ((jb == 0) | (jb == cur) | (jb == cur - 1)).astype(jnp.float32)
    score = jnp.where(jb * SLC_BLOCK <= tq, imp + FORCE_BONUS * forced, NEG_INF)
    rank = jnp.zeros((N_SLC, TQ), jnp.float32)
    for i in range(N_SLC):
        si = score[i:i + 1]
        beats = (si > score) | ((si == score) & (jb > i))
        rank = rank + beats.astype(jnp.float32)
    sel = _bf16((rank < SLC_TOPN).astype(jnp.float32))

    def logits(k_ref, first_tile, n_tiles, tile_of_dist):
        off = pl.multiple_of(first_tile * TQ, TQ)
        st = _dot(k_ref[pl.ds(off, n_tiles * TQ), :], qt)
        tiles = [tile_of_dist(qi - (first_tile + u)) for u in range(n_tiles)]
        out = []
        for r in range(r4):
            bias = jnp.concatenate([bt_ref[r, tl] for tl in tiles], axis=0)
            out.append(st[:, r * TQ:(r + 1) * TQ] * QK_SCALE + bias)
        return off, out

    slc_tile = lambda d: jnp.where(d < 0, BIAS_TILE_MASKED, jnp.minimum(d, BIAS_TILE_FAR))
    keys = SLC_GROUP * TQ
    acc_ref[...] = jnp.zeros_like(acc_ref)

    def slc_body(it, carry):
        m, l = carry
        off, chunks = logits(ks_ref, it * SLC_GROUP, SLC_GROUP, slc_tile)
        selb = (_dot(exp_ref[pl.ds(off, keys), :], sel) - 1.0) * (-NEG_INF)
        p, alpha, m, l = _softmax_step([s + selb for s in chunks], m, l, keys)
        acc_ref[...] = acc_ref[...] * alpha + _dot(vst_ref[:, pl.ds(off, keys)], p)
        return m, l

    init = (jnp.full((1, r4 * TQ), -jnp.inf, jnp.float32), jnp.zeros((1, r4 * TQ), jnp.float32))
    _, l = lax.fori_loop(0, qi // SLC_GROUP + 1, slc_body, init)
    oslc_ref[...] = acc_ref[...] * (1.0 / l)

    far_d = WINDOW // TQ
    win_tile = lambda d: jnp.where(d < 0, BIAS_TILE_MASKED, jnp.where(
        d < BIAS_TILE_FAR, d, jnp.where(d < far_d, BIAS_TILE_FAR, BIAS_TILE_EDGE)))
    off, chunks = logits(kw_ref, jnp.maximum(qi - far_d, 0), WIN_TILES, win_tile)
    p, _, _, l = _softmax_step(chunks, init[0], init[1], WIN_TILES * TQ)
    acc_ref[...] = _dot(vwt_ref[:, pl.ds(off, WIN_TILES * TQ)], p) * (1.0 / l)

    gt = _sigmoid(zg_ref[...]).T
    for r in range(r4):
        lanes = slice(r * TQ, (r + 1) * TQ)
        o = (gt[3 * r:3 * r + 1] * ocmp_ref[:, lanes] + gt[3 * r + 1:3 * r + 2] * oslc_ref[:, lanes]
             + gt[3 * r + 2:3 * r + 3] * acc_ref[:, lanes])
        o_ref[:, r * HEAD_DIM:(r + 1) * HEAD_DIM] = o.T


def _nsa_consts(seq):
    nc = np.arange(N_CMP_PAD)
    blk_start = nc * CMP_STRIDE
    blk_end = blk_start + CMP_BLOCK - 1
    sel_start = np.arange(N_SLC) * SLC_BLOCK
    ovl = ((blk_start[None, :] <= sel_start[:, None] + SLC_BLOCK - 1) & (blk_end[None, :] >= sel_start[:, None])
           & (nc[None, :] < N_CMP_PAD - 1))
    expand = (np.arange(seq)[:, None] // SLC_BLOCK) == np.arange(N_SLC)[None, :]
    return jnp.asarray(ovl, jnp.bfloat16), jnp.asarray(expand, jnp.bfloat16)


def _nsa_attn(za, vt, z_gate, kc, vct, bias_c, bias_t, batch, seq):
    nq = seq // TQ
    ovl, expand = _nsa_consts(seq)
    cb = lambda c: c // HEAD_DIM
    keys = lambda col: pl.BlockSpec((seq, HEAD_DIM), lambda b, g, i: (b, cb(col) + g))
    vals = lambda h0: pl.BlockSpec((None, None, HEAD_DIM, seq), lambda b, g, i: (b, h0 + g, 0, 0))
    cmp = pl.BlockSpec((None, None, N_CMP_PAD, HEAD_DIM), lambda b, g, i: (b, g, 0, 0))
    qw = NSA_REP * HEAD_DIM
    wide = pltpu.VMEM((HEAD_DIM, NSA_REP * TQ), jnp.float32)
    return pl.pallas_call(
        _nsa_attn_kernel,
        grid=(batch, NSA_KV_HEADS, nq),
        in_specs=[
            pl.BlockSpec((TQ, qw), lambda b, g, i: (b * nq + i, COL_QC // qw + g)),
            cmp, cmp, keys(COL_KCS), vals(V_HEAD_SLC), keys(COL_KCW), vals(V_HEAD_WIN),
            pl.BlockSpec((TQ, LANES), lambda b, g, i: (b * nq + i, COL_GATE // LANES + 1 + g)),
            pl.BlockSpec((NSA_REP, None, N_CMP_PAD, TQ), lambda b, g, i: (g, i, 0, 0)),
            pl.BlockSpec((NSA_REP, N_BIAS_TILES, TQ, TQ), lambda b, g, i: (g, 0, 0, 0)),
            pl.BlockSpec((N_SLC, N_CMP_PAD), lambda b, g, i: (0, 0)),
            pl.BlockSpec((seq, N_SLC), lambda b, g, i: (0, 0)),
        ],
        out_specs=pl.BlockSpec((TQ, qw), lambda b, g, i: (b * nq + i, g)),
        out_shape=jax.ShapeDtypeStruct((batch * seq, NSA_WIDTH), jnp.float32),
        scratch_shapes=[pltpu.VMEM((HEAD_DIM, NSA_REP * TQ), jnp.bfloat16), wide, wide, wide],
        compiler_params=_cparams("parallel", "parallel", "arbitrary"),
        name="nsa_attention",
    )(za, kc, vct, za, vt, za, vt, z_gate, bias_c, bias_t, ovl, expand)


def _mixnorm_kernel(oa_ref, ob_ref, oc_ref, ga_ref, gc_ref, o_ref):
    oa = oa_ref[...]
    oc = oc_ref[...]
    o_ref[:, :FOX_WIDTH] = _bf16(oa * _rms_scale(oa) * ga_ref[...])
    o_ref[:, FOX_WIDTH:FOX_WIDTH + S5_WIDTH] = ob_ref[...]
    o_ref[:, FOX_WIDTH + S5_WIDTH:] = _bf16(oc * _rms_scale(oc) * gc_ref[...])


def _mixnorm(o_a, o_b, o_c, g_a, g_c, tm=ROW_TILE):
    t = o_a.shape[0]
    row = lambda w: pl.BlockSpec((tm, w), lambda i: (i, 0))
    vec = lambda w: pl.BlockSpec((1, w), lambda i: (0, 0))
    return pl.pallas_call(
        _mixnorm_kernel,
        grid=(t // tm,),
        in_specs=[row(FOX_WIDTH), row(S5_WIDTH), row(NSA_WIDTH), vec(FOX_WIDTH), vec(NSA_WIDTH)],
        out_specs=row(D_MODEL),
        out_shape=jax.ShapeDtypeStruct((t, D_MODEL), jnp.bfloat16),
        compiler_params=_cparams("parallel"),
        name="mixer_norm_concat",
    )(o_a, o_b, o_c, g_a.reshape(1, -1), g_c.reshape(1, -1))


def _split_w_in(w):
    f, kv = FOX_WIDTH, NSA_KV_WIDTH
    o = 0
    qa, o = w[:, o:o + f], o + f
    ka, o = w[:, o:o + f], o + f
    va, o = w[:, o:o + f], o + f
    zf, o = w[:, o:o + FOX_HEADS], o + FOX_HEADS
    ub, o = w[:, o:o + S5_WIDTH], o + S5_WIDTH
    qc, o = w[:, o:o + NSA_WIDTH], o + NSA_WIDTH
    kcc_vcc, o = w[:, o:o + 2 * kv], o + 2 * kv
    rest, o = w[:, o:o + 4 * kv], o + 4 * kv
    zg = w[:, o:o + 3 * NSA_HEADS]
    kcs, vcs, kcw, vcw = (rest[:, n * kv:(n + 1) * kv] for n in range(4))
    w_attn = _bf16(jnp.concatenate([qa, ka, qc, kcs, kcw, va, vcs, vcw], axis=1))
    per_g = 3 * NSA_REP
    gates = [jnp.pad(zf, ((0, 0), (0, LANES - FOX_HEADS)))]
    for g in range(NSA_KV_HEADS):
        gates.append(jnp.pad(zg[:, g * per_g:(g + 1) * per_g], ((0, 0), (0, LANES - per_g))))
    w_f32 = _bf16(jnp.concatenate([ub, kcc_vcc] + gates, axis=1))
    return w_attn, w_f32


def _block_diag(blocks):
    j, n, a, b = blocks.shape
    eye = jnp.eye(n, dtype=blocks.dtype)
    return (blocks[:, :, :, None, :] * eye[None, :, None, :, None]).reshape(j, n * a, n * b)


def _s5_weights(bb_re, bb_im, c_re, c_im):
    n = S5_SLAB // S5_GROUP_CH
    layers = bb_re.shape[0]

    def slabs(m, a, b):
        d = _block_diag(m.reshape(layers * N_S5_SLABS, n, a, b))
        return _bf16(d.reshape(layers, N_S5_SLABS, n * a, n * b))

    inp = lambda m: slabs(m.transpose(0, 1, 3, 2), S5_GROUP_CH, S5_STATE)
    out = lambda m: slabs(m.transpose(0, 1, 3, 2), S5_STATE, S5_GROUP_CH)
    return inp(bb_re), inp(bb_im), out(c_re), out(c_im)


def kernel(x, w_in, b_forget, s5_a_re, s5_a_im, s5_log_dt, s5_b_re, s5_b_im, s5_c_re, s5_c_im, s5_d, s5_w_glu,
           cmp_pe_k, cmp_pe_v, cmp_w_k, cmp_w_v, rel_bias, g_out_fox, g_out_s5, g_out_nsa, w_out, g_pre_mix,
           g_post_mix, g_pre_ffn, g_post_ffn, w_up, conv_w, conv_b, w_down):
    batch, seq, d = x.shape
    t = batch * seq
    xf = x.reshape(t, d)

    bias_c, bias_t = _nsa_bias_tables(rel_bias, seq)
    bb_re, bb_im, pw_re, pw_im = _s5_discretize(s5_a_re, s5_a_im, s5_log_dt, s5_b_re, s5_b_im)
    s5w = _s5_weights(bb_re, bb_im, s5_c_re, s5_c_im)

    hn = _rmsnorm_cast(xf, g_pre_mix[0])
    for l in range(DEPTH):
        w_attn, w_f32 = _split_w_in(w_in[l])
        za = _matmul(hn, w_attn, IN_PROJ_TN, jnp.bfloat16, "in_proj_attn")
        zb = _matmul(hn, w_f32, N_F32 // 3, jnp.float32, "in_proj_f32")

        ccol, crow = _fox_gate(zb, b_forget[l], batch, seq)
        vt = _values_t(za, batch, seq)
        o_a = _fox_attn(za, vt, ccol, crow, batch, seq)

        y_b = _s5_scan(zb, *(w[l] for w in s5w), s5_d[l], pw_re[l], pw_im[l], batch, seq)
        o_b = _s5_glu(y_b, _bf16(s5_w_glu[l]), g_out_s5[l])

        chunks = _compress_chunks(zb, batch, seq)
        kc = _nsa_compress(chunks, 0, cmp_pe_k[l], cmp_w_k[l], False)
        vct = _nsa_compress(chunks, NSA_KV_HEADS, cmp_pe_v[l], cmp_w_v[l], True)
        o_c = _nsa_attn(za, vt, zb, kc, vct, bias_c, bias_t, batch, seq)

        mixed = _mixnorm(o_a, o_b, o_c, g_out_fox[l], g_out_nsa[l])
        xf, hn = _matmul_resid_cols(mixed, _bf16(w_out[l]), xf, g_post_mix[l], g_pre_ffn[l], "out_proj")

        wg, wv = _bf16(w_up[l][:, :D_FF]), _bf16(w_up[l][:, D_FF:])
        cwg, cwv = conv_w[l][:, :D_FF], conv_w[l][:, D_FF:]
        cbg, cbv = conv_b[l][None, :D_FF], conv_b[l][None, D_FF:]
        act = _ffn_up(hn, wg, wv, cwg, cwv, cbg, cbv, seq)
        g_next = g_pre_mix[l + 1] if l + 1 < DEPTH else jnp.ones((d,), jnp.float32)
        xf, hn = _matmul_resid(act, _bf16(w_down[l]), xf, g_post_ffn[l], g_next, "ffn_down")
    return xf.reshape(batch, seq, d)
```

```python
import functools
import math

import numpy as np
import jax
import jax.numpy as jnp
from jax import lax
from jax.experimental import pallas as pl
from jax.experimental.pallas import tpu as pltpu

D_MODEL = 4096
DEPTH = 4
HEAD_DIM = 128
FOX_WIDTH = 1024
FOX_HEADS = 8
S5_WIDTH = 1024
S5_GROUP_CH = 16
S5_GROUPS = 64
S5_STATE = 64
NSA_WIDTH = 2048
NSA_HEADS = 16
NSA_KV_HEADS = 4
NSA_REP = 4
NSA_KV_WIDTH = 512
CMP_BLOCK = 32
CMP_STRIDE = 16
SLC_BLOCK = 64
SLC_TOPN = 16
WINDOW = 512
FORCE_BONUS = 1e3
REL_BUCKETS = 32
REL_EXACT = 16
REL_MAX_DIST = 128
D_FF = 11008
RMS_EPS = 1e-6
NEG_INF = -1e30
LOG2E = 1.4426950408889634
QK_SCALE = HEAD_DIM ** -0.5 * LOG2E

LANES = 128
SUBLANES = 8
VMEM_LIMIT = 60 * 1024 * 1024

S5_SLAB = 256
S5_SLAB_STATE = S5_SLAB // S5_GROUP_CH * S5_STATE
N_S5_SLABS = S5_WIDTH // S5_SLAB
COL_QA, COL_KA, COL_QC, COL_KCS, COL_KCW, COL_V = 0, 1024, 2048, 4096, 4608, 5120
N_ATTN = 7168
V_HEAD_FOX, V_HEAD_SLC, V_HEAD_WIN, N_V_HEADS = 0, FOX_HEADS, FOX_HEADS + NSA_KV_HEADS, FOX_HEADS + 2 * NSA_KV_HEADS
COL_UB, COL_KCC, COL_VCC, COL_GATE = 0, 1024, 1536, 2048
N_GATE = LANES * (1 + NSA_KV_HEADS)
N_F32 = COL_GATE + N_GATE
TQ = 128
SLC_GROUP = 4
WIN_TILES = WINDOW // TQ + 1
BIAS_TILE_FAR, BIAS_TILE_EDGE, BIAS_TILE_MASKED, N_BIAS_TILES = 2, 3, 4, 5
FOX_TQ = 512
N_CMP_PAD = 128
N_SLC = 32
ROW_TILE = 512
IN_PROJ_TN = 1024
RESID_TK = 1024
RESID_COLS = 512
RESID_ROWS = 128
FFN_TN = 512
FFN_CHUNK = 256


def _cparams(*sem):
    return pltpu.CompilerParams(dimension_semantics=sem, vmem_limit_bytes=VMEM_LIMIT)


def _bf16(x):
    return x.astype(jnp.bfloat16)


def _dot(a, b):
    return jnp.dot(a, b, preferred_element_type=jnp.float32)


def _rms_scale(x):
    return lax.rsqrt(jnp.mean(x * x, axis=-1, keepdims=True) + RMS_EPS)


def _gelu_tanh(x):
    return 0.5 * x * (1.0 + jnp.tanh(math.sqrt(2.0 / math.pi) * (x + 0.044715 * (x * x * x))))


def _sigmoid(x):
    return 1.0 / (1.0 + jnp.exp(-x))


def _rmsnorm_cast_kernel(x_ref, g_ref, o_ref):
    x = x_ref[...]
    o_ref[...] = _bf16(x * _rms_scale(x) * g_ref[...])


def _rmsnorm_cast(x, g, tm=ROW_TILE):
    t, d = x.shape
    return pl.pallas_call(
        _rmsnorm_cast_kernel,
        grid=(t // tm,),
        in_specs=[pl.BlockSpec((tm, d), lambda i: (i, 0)), pl.BlockSpec((1, d), lambda i: (0, 0))],
        out_specs=pl.BlockSpec((tm, d), lambda i: (i, 0)),
        out_shape=jax.ShapeDtypeStruct((t, d), jnp.bfloat16),
        compiler_params=_cparams("parallel"),
        name="rmsnorm_cast",
    )(x, g.reshape(1, d))


def _matmul_kernel(a_ref, w_ref, o_ref):
    o_ref[...] = _dot(a_ref[...], w_ref[...]).astype(o_ref.dtype)


def _matmul(a, w, tn, out_dtype, name, tm=ROW_TILE):
    t, k = a.shape
    n = w.shape[1]
    return pl.pallas_call(
        _matmul_kernel,
        grid=(t // tm, n // tn),
        in_specs=[pl.BlockSpec((tm, k), lambda i, j: (i, 0)), pl.BlockSpec((k, tn), lambda i, j: (0, j))],
        out_specs=pl.BlockSpec((tm, tn), lambda i, j: (i, j)),
        out_shape=jax.ShapeDtypeStruct((t, n), out_dtype),
        compiler_params=_cparams("parallel", "arbitrary"),
        name=name,
    )(a, w)


def _resid_epilogue(x_ref, gpost_ref, gnext_ref, xo_ref, hn_ref):
    tm = xo_ref.shape[0]
    for r in range(tm // RESID_ROWS):
        rows = slice(r * RESID_ROWS, (r + 1) * RESID_ROWS)
        f = xo_ref[rows, :]
        xn = x_ref[rows, :] + f * _rms_scale(f) * gpost_ref[...]
        xo_ref[rows, :] = xn
        hn_ref[rows, :] = _bf16(xn * _rms_scale(xn) * gnext_ref[...])


def _residual_copy(x_hbm, xbuf_ref, sem):
    tm = xbuf_ref.shape[0]
    rows = pl.ds(pl.multiple_of(pl.program_id(0) * tm, tm), tm)
    return pltpu.make_async_copy(x_hbm.at[rows, :], xbuf_ref, sem)


def _matmul_resid_cols_kernel(a_ref, w_ref, x_hbm, gpost_ref, gnext_ref, xo_ref, hn_ref, xbuf_ref, sem):
    j = pl.program_id(1)
    tn = w_ref.shape[1]
    pl.when(j == 0)(lambda: _residual_copy(x_hbm, xbuf_ref, sem).start())
    xo_ref[:, pl.ds(pl.multiple_of(j * tn, tn), tn)] = _dot(a_ref[...], w_ref[...])

    @pl.when(j == pl.num_programs(1) - 1)
    def _():
        _residual_copy(x_hbm, xbuf_ref, sem).wait()
        _resid_epilogue(xbuf_ref, gpost_ref, gnext_ref, xo_ref, hn_ref)


def _matmul_resid_cols(a, w, x, g_post, g_next, name, tm=ROW_TILE, tn=RESID_COLS):
    t, kk = a.shape
    d = w.shape[1]
    return pl.pallas_call(
        _matmul_resid_cols_kernel,
        grid=(t // tm, d // tn),
        in_specs=[
            pl.BlockSpec((tm, kk), lambda i, j: (i, 0)),
            pl.BlockSpec((kk, tn), lambda i, j: (0, j)),
            pl.BlockSpec(memory_space=pl.ANY),
            pl.BlockSpec((1, d), lambda i, j: (0, 0)),
            pl.BlockSpec((1, d), lambda i, j: (0, 0)),
        ],
        out_specs=[pl.BlockSpec((tm, d), lambda i, j: (i, 0)), pl.BlockSpec((tm, d), lambda i, j: (i, 0))],
        out_shape=[jax.ShapeDtypeStruct((t, d), jnp.float32), jax.ShapeDtypeStruct((t, d), jnp.bfloat16)],
        scratch_shapes=[pltpu.VMEM((tm, d), jnp.float32), pltpu.SemaphoreType.DMA(())],
        compiler_params=_cparams("parallel", "arbitrary"),
        name=name,
    )(a, w, x, g_post.reshape(1, d), g_next.reshape(1, d))


def _matmul_resid_kernel(k_total, a_ref, w_ref, x_hbm, gpost_ref, gnext_ref, xo_ref, hn_ref, xbuf_ref, sem):
    k = pl.program_id(1)
    last = pl.num_programs(1) - 1
    tm, d = xo_ref.shape
    tk = a_ref.shape[1]
    tail = k_total % tk

    @pl.when(k == 0)
    def _():
        _residual_copy(x_hbm, xbuf_ref, sem).start()
        xo_ref[...] = jnp.zeros_like(xo_ref)

    def accumulate(masked):
        a = a_ref[...]
        if masked:
            acol = lax.broadcasted_iota(jnp.int32, a.shape, 1) < tail
            a = _bf16(jnp.where(acol, a.astype(jnp.float32), 0.0))
            wrow = lax.broadcasted_iota(jnp.int32, (tk, RESID_COLS), 0) < tail
        for n in range(d // RESID_COLS):
            cols = slice(n * RESID_COLS, (n + 1) * RESID_COLS)
            w = w_ref[:, cols]
            if masked:
                w = _bf16(jnp.where(wrow, w.astype(jnp.float32), 0.0))
            xo_ref[:, cols] += _dot(a, w)

    if tail == 0:
        accumulate(False)
    else:
        pl.when(k < last)(lambda: accumulate(False))
        pl.when(k == last)(lambda: accumulate(True))

    @pl.when(k == last)
    def _():
        _residual_copy(x_hbm, xbuf_ref, sem).wait()
        _resid_epilogue(xbuf_ref, gpost_ref, gnext_ref, xo_ref, hn_ref)


def _matmul_resid(a, w, x, g_post, g_next, name, tm=ROW_TILE, tk=RESID_TK):
    t, kk = a.shape
    d = w.shape[1]
    return pl.pallas_call(
        functools.partial(_matmul_resid_kernel, kk),
        grid=(t // tm, pl.cdiv(kk, tk)),
        in_specs=[
            pl.BlockSpec((tm, tk), lambda i, k: (i, k)),
            pl.BlockSpec((tk, d), lambda i, k: (k, 0)),
            pl.BlockSpec(memory_space=pl.ANY),
            pl.BlockSpec((1, d), lambda i, k: (0, 0)),
            pl.BlockSpec((1, d), lambda i, k: (0, 0)),
        ],
        out_specs=[pl.BlockSpec((tm, d), lambda i, k: (i, 0)), pl.BlockSpec((tm, d), lambda i, k: (i, 0))],
        out_shape=[jax.ShapeDtypeStruct((t, d), jnp.float32), jax.ShapeDtypeStruct((t, d), jnp.bfloat16)],
        scratch_shapes=[pltpu.VMEM((tm, d), jnp.float32), pltpu.SemaphoreType.DMA(())],
        compiler_params=_cparams("parallel", "arbitrary"),
        name=name,
    )(a, w, x, g_post.reshape(1, d), g_next.reshape(1, d))


def _ffn_up_kernel(seq, h_ref, halo_ref, wg_ref, wv_ref, cwg_ref, cwv_ref, cbg_ref, cbv_ref, o_ref):
    i = pl.program_id(0)
    tm, tn = o_ref.shape
    h = h_ref[...]
    keep = ((i * tm) % seq != 0).astype(jnp.float32)
    halo = halo_ref[...]
    row = lax.broadcasted_iota(jnp.int32, (tm, 1), 0)

    def conv(w_ref, cw_ref, cb_ref, cols):
        w = w_ref[:, cols]
        u = _dot(h, w)
        uh = _dot(halo, w) * keep
        p1 = jnp.where(row == 0, uh[7:8], pltpu.roll(u, 1, 0))
        p2 = jnp.where(row == 0, uh[6:7], jnp.where(row == 1, uh[7:8], pltpu.roll(u, 2, 0)))
        cw = cw_ref[:, cols]
        return cb_ref[:, cols] + cw[0:1] * p2 + cw[1:2] * p1 + cw[2:3] * u

    for c in range(tn // FFN_CHUNK):
        cols = slice(c * FFN_CHUNK, (c + 1) * FFN_CHUNK)
        gate = conv(wg_ref, cwg_ref, cbg_ref, cols)
        val = conv(wv_ref, cwv_ref, cbv_ref, cols)
        o_ref[:, cols] = _bf16(_gelu_tanh(gate) * val)


def _ffn_up(hn, wg, wv, cwg, cwv, cbg, cbv, seq, tm=ROW_TILE, tn=FFN_TN):
    t, d = hn.shape
    f = wg.shape[1]
    hb = tm // SUBLANES
    return pl.pallas_call(
        functools.partial(_ffn_up_kernel, seq),
        grid=(t // tm, pl.cdiv(f, tn)),
        in_specs=[
            pl.BlockSpec((tm, d), lambda i, j: (i, 0)),
            pl.BlockSpec((SUBLANES, d), lambda i, j: (jnp.maximum(i * hb - 1, 0), 0)),
            pl.BlockSpec((d, tn), lambda i, j: (0, j)),
            pl.BlockSpec((d, tn), lambda i, j: (0, j)),
            pl.BlockSpec((3, tn), lambda i, j: (0, j)),
            pl.BlockSpec((3, tn), lambda i, j: (0, j)),
            pl.BlockSpec((1, tn), lambda i, j: (0, j)),
            pl.BlockSpec((1, tn), lambda i, j: (0, j)),
        ],
        out_specs=pl.BlockSpec((tm, tn), lambda i, j: (i, j)),
        out_shape=jax.ShapeDtypeStruct((t, f), jnp.bfloat16),
        compiler_params=_cparams("parallel", "arbitrary"),
        name="ffn_up_conv_geglu",
    )(hn, hn, wg, wv, cwg, cwv, cbg, cbv)


def _softmax_step(s_chunks, m, l, keys):
    ps, alphas, ms, ls = [], [], [], []
    for c, s in enumerate(s_chunks):
        lanes = slice(c * LANES, (c + 1) * LANES)
        m_old = m[:, lanes]
        m_new = jnp.maximum(m_old, jnp.max(s, axis=0, keepdims=True))
        alpha = jnp.exp2(m_old - m_new)
        p = jnp.exp2(s - m_new)
        ls.append(alpha * l[:, lanes] + jnp.sum(p, axis=0, keepdims=True))
        ms.append(m_new)
        alphas.append(alpha)
        p = _bf16(p)
        if s.shape[0] < keys:
            p = jnp.concatenate([p, jnp.zeros((keys - s.shape[0], LANES), jnp.bfloat16)], axis=0)
        ps.append(p)
    cat = lambda xs: jnp.concatenate(xs, axis=1)
    return cat(ps), cat(alphas), cat(ms), cat(ls)


def _split3(x):
    hi = _bf16(x)
    r = x - hi.astype(jnp.float32)
    mid = _bf16(r)
    lo = _bf16(r - mid.astype(jnp.float32))
    return hi, mid, lo


def _fox_gate_kernel(zf_ref, bf_ref, ccol_ref, crow_ref):
    seq = zf_ref.shape[0]
    blk = 256
    x = zf_ref[...] + bf_ref[...]
    logf = -(jnp.maximum(-x, 0.0) + jnp.log(1.0 + jnp.exp(-jnp.abs(x))))
    r = lax.broadcasted_iota(jnp.int32, (blk, blk), 0)
    c = lax.broadcasted_iota(jnp.int32, (blk, blk), 1)
    tri = _bf16((c <= r).astype(jnp.float32))
    carry = jnp.zeros((1, LANES), jnp.float32)
    for b in range(seq // blk):
        hi, mid, lo = _split3(logf[b * blk:(b + 1) * blk])
        cs = _dot(tri, hi) + _dot(tri, mid) + _dot(tri, lo) + carry
        ccol_ref[b * blk:(b + 1) * blk, :] = cs
        carry = cs[blk - 1:blk]
    crow_ref[...] = ccol_ref[...].T


def _fox_gate(zb, b_f, batch, seq):
    bf = jnp.zeros((1, LANES), jnp.float32).at[0, :FOX_HEADS].set(b_f)
    return pl.pallas_call(
        _fox_gate_kernel,
        grid=(batch,),
        in_specs=[pl.BlockSpec((seq, LANES), lambda b: (b, COL_GATE // LANES)),
                  pl.BlockSpec((1, LANES), lambda b: (0, 0))],
        out_specs=[pl.BlockSpec((seq, LANES), lambda b: (b, 0)), pl.BlockSpec((None, LANES, seq), lambda b: (b, 0, 0))],
        out_shape=[jax.ShapeDtypeStruct((batch * seq, LANES), jnp.float32),
                   jax.ShapeDtypeStruct((batch, LANES, seq), jnp.float32)],
        compiler_params=_cparams("parallel"),
        name="fox_gate_cumsum",
    )(zb, bf)


def _fox_attn_kernel(q_ref, k_ref, vt_ref, cq_ref, ck_ref, o_ref, acc_ref):
    qi = pl.program_id(2)
    tq = q_ref.shape[0]
    nch = tq // LANES
    qt = _bf16(q_ref[...].astype(jnp.float32).T)
    cq = cq_ref[...] * LOG2E
    acc_ref[...] = jnp.zeros_like(acc_ref)

    def step(kt, m, l, diagonal):
        off = pl.multiple_of(kt * tq, tq)
        st = _dot(k_ref[pl.ds(off, tq), :], qt)
        ck = ck_ref[pl.ds(off, tq), :] * LOG2E
        chunks = []
        for c in range(nch):
            lanes = slice(c * LANES, (c + 1) * LANES)
            if diagonal:
                rows = (c + 1) * LANES
                s = st[:rows, lanes] * QK_SCALE + (cq[:, lanes] - ck[:rows])
                krow = lax.broadcasted_iota(jnp.int32, (rows, LANES), 0)
                qcol = c * LANES + lax.broadcasted_iota(jnp.int32, (rows, LANES), 1)
                s = jnp.where(krow <= qcol, s, NEG_INF)
            else:
                s = st[:, lanes] * QK_SCALE + (cq[:, lanes] - ck)
            chunks.append(s)
        p, alpha, m, l = _softmax_step(chunks, m, l, tq)
        acc_ref[...] = acc_ref[...] * alpha + _dot(vt_ref[:, pl.ds(off, tq)], p)
        return m, l

    init = (jnp.full((1, tq), -jnp.inf, jnp.float32), jnp.zeros((1, tq), jnp.float32))
    m, l = lax.fori_loop(0, qi, lambda kt, c: step(kt, c[0], c[1], False), init)
    _, l = step(qi, m, l, True)
    o_ref[...] = (acc_ref[...] * (1.0 / l)).T


def _values_t(za, batch, seq):
    v = za[:, COL_V:COL_V + N_V_HEADS * HEAD_DIM]
    return v.reshape(batch, seq, N_V_HEADS, HEAD_DIM).transpose(0, 2, 3, 1)


def _fox_attn(za, vt, ccol, crow, batch, seq, tq=FOX_TQ):
    nq = seq // tq
    ck = ccol[:, :FOX_HEADS].reshape(batch, seq, FOX_HEADS).transpose(0, 2, 1).reshape(batch, FOX_HEADS, seq, 1)
    cq = crow[:, :FOX_HEADS, :].reshape(batch, FOX_HEADS, 1, seq)
    cb = lambda c: c // HEAD_DIM
    return pl.pallas_call(
        _fox_attn_kernel,
        grid=(batch, FOX_HEADS, nq),
        in_specs=[
            pl.BlockSpec((tq, HEAD_DIM), lambda b, h, i: (b * nq + i, cb(COL_QA) + h)),
            pl.BlockSpec((seq, HEAD_DIM), lambda b, h, i: (b, cb(COL_KA) + h)),
            pl.BlockSpec((None, None, HEAD_DIM, seq), lambda b, h, i: (b, V_HEAD_FOX + h, 0, 0)),
            pl.BlockSpec((None, None, 1, tq), lambda b, h, i: (b, h, 0, i)),
            pl.BlockSpec((None, None, seq, 1), lambda b, h, i: (b, h, 0, 0)),
        ],
        out_specs=pl.BlockSpec((tq, HEAD_DIM), lambda b, h, i: (b * nq + i, h)),
        out_shape=jax.ShapeDtypeStruct((batch * seq, FOX_WIDTH), jnp.float32),
        scratch_shapes=[pltpu.VMEM((HEAD_DIM, tq), jnp.float32)],
        compiler_params=_cparams("parallel", "parallel", "arbitrary"),
        name="fox_attention",
    )(za, za, vt, cq, ck)


def _s5_disc_kernel(are_ref, aim_ref, ldt_ref, bre_ref, bim_ref, bbre_ref, bbim_ref, pwre_ref, pwim_ref):
    lam_re = jnp.minimum(are_ref[...], -1e-4)
    lam_im = aim_ref[...]
    dt = jnp.exp(ldt_ref[...])
    mag = jnp.exp(lam_re * dt)
    ang = lam_im * dt
    lb_re, lb_im = mag * jnp.cos(ang), mag * jnp.sin(ang)
    den = lam_re * lam_re + lam_im * lam_im
    nr, ni = lb_re - 1.0, lb_im
    coef_re = (nr * lam_re + ni * lam_im) / den
    coef_im = (ni * lam_re - nr * lam_im) / den
    b_re, b_im = bre_ref[...], bim_ref[...]
    bbre_ref[...] = coef_re * b_re - coef_im * b_im
    bbim_ref[...] = coef_re * b_im + coef_im * b_re
    p_re, p_im = lb_re, lb_im
    for i in range(SUBLANES):
        pwre_ref[i:i + 1, :] = p_re
        pwim_ref[i:i + 1, :] = p_im
        p_re, p_im = p_re * lb_re - p_im * lb_im, p_re * lb_im + p_im * lb_re


def _s5_discretize(a_re, a_im, log_dt, b_re, b_im):
    l, g, p = a_re.shape
    h = b_re.shape[-1]
    n = l * g * p
    row = lambda a: a.reshape(1, n)
    ldt = jnp.broadcast_to(log_dt[:, :, None], (l, g, p))
    bt = lambda b: b.reshape(n, h).T
    full = lambda r: pl.BlockSpec((r, n), lambda: (0, 0))
    bbre, bbim, pwre, pwim = pl.pallas_call(
        _s5_disc_kernel,
        in_specs=[full(1), full(1), full(1), full(h), full(h)],
        out_specs=[full(h), full(h), full(SUBLANES), full(SUBLANES)],
        out_shape=[jax.ShapeDtypeStruct((h, n), jnp.float32)] * 2 + [jax.ShapeDtypeStruct((SUBLANES, n), jnp.float32)] * 2,
        compiler_params=pltpu.CompilerParams(vmem_limit_bytes=VMEM_LIMIT),
        name="s5_discretize",
    )(row(a_re), row(a_im), row(ldt), bt(b_re), bt(b_im))
    unb = lambda b: b.T.reshape(l, g, p, h)
    unp = lambda q: q.reshape(SUBLANES, l, g * p).transpose(1, 0, 2)
    return unb(bbre), unb(bbim), unp(pwre), unp(pwim)


def _s5_scan_kernel(u_ref, bre_ref, bim_ref, cre_ref, cim_ref, d_ref, pwre_ref, pwim_ref, y_ref, hre_ref, him_ref):
    seq = u_ref.shape[0]
    u = u_ref[...]
    ub = _bf16(u)
    for ref, b_ref in ((hre_ref, bre_ref), (him_ref, bim_ref)):
        bu = _dot(ub, b_ref[...])
        for c in range(ref.shape[0]):
            ref[c] = bu[:, c * LANES:(c + 1) * LANES]

    pw_re, pw_im = pwre_ref[...], pwim_ref[...]
    sub = lax.broadcasted_iota(jnp.int32, pw_re.shape, 0)

    def cmul(a, b):
        return a[0] * b[0] - a[1] * b[1], a[0] * b[1] + a[1] * b[0]

    def cmadd(x, a, b):
        return x[0] + a[0] * b[0] - a[1] * b[1], x[1] + a[0] * b[1] + a[1] * b[0]

    lb = (pw_re[0:1], pw_im[0:1])
    lb8 = (pw_re[7:8], pw_im[7:8])
    lb16 = cmul(lb8, lb8)
    lb32 = cmul(lb16, lb16)
    masked = lambda p, k: (jnp.where(sub >= k, p[0], 0.0), jnp.where(sub >= k, p[1], 0.0))
    tile_steps = [(1, masked(lb8, 1)), (2, masked(lb16, 2)), (4, masked(lb32, 4))]
    q, qrow = (jnp.where(sub == 0, 1.0, 0.0), jnp.zeros_like(pw_re)), (jnp.ones_like(lb[0]), jnp.zeros_like(lb[0]))
    for j in range(1, SUBLANES):
        qrow = cmul(qrow, lb8)
        q = (jnp.where(sub == j, qrow[0], q[0]), jnp.where(sub == j, qrow[1], q[1]))
    blk = SUBLANES * SUBLANES

    n_chunks = pw_re.shape[1] // LANES
    lane = lambda p, c: (p[0][:, c * LANES:(c + 1) * LANES], p[1][:, c * LANES:(c + 1) * LANES])
    first = lambda z: z[:SUBLANES, :LANES]
    sub1 = first(sub)

    def block(b, carry):
        base = pl.multiple_of(b * blk, blk)
        outs = []
        for c in range(n_chunks):
            rows = lambda i: pl.ds(base + i, SUBLANES, stride=SUBLANES)
            y = [(hre_ref[c, rows(0), :], him_ref[c, rows(0), :])]
            for i in range(1, SUBLANES):
                y.append(cmadd((hre_ref[c, rows(i), :], him_ref[c, rows(i), :]), lane(lb, c), y[-1]))
            s = y[-1]
            for k, a in tile_steps:
                s = cmadd(s, lane(a, c), (pltpu.roll(s[0], k, 0), pltpu.roll(s[1], k, 0)))
            before = (jnp.where(sub1 == 0, 0.0, pltpu.roll(s[0], 1, 0)),
                      jnp.where(sub1 == 0, 0.0, pltpu.roll(s[1], 1, 0)))
            before = cmadd(before, lane(q, c), lane(carry, c))
            for i in range(SUBLANES):
                h = cmadd(y[i], lane((pw_re[i:i + 1], pw_im[i:i + 1]), c), before)
                hre_ref[c, rows(i), :] = h[0]
                him_ref[c, rows(i), :] = h[1]
            outs.append((h[0][SUBLANES - 1:SUBLANES], h[1][SUBLANES - 1:SUBLANES]))
        return (jnp.concatenate([o[0] for o in outs], axis=1), jnp.concatenate([o[1] for o in outs], axis=1))

    zero = jnp.zeros((1, pw_re.shape[1]), jnp.float32)
    lax.fori_loop(0, seq // blk, block, (zero, zero))

    wide = lambda ref: jnp.concatenate([_bf16(ref[c]) for c in range(n_chunks)], axis=1)
    y = _dot(wide(hre_ref), cre_ref[...]) - _dot(wide(him_ref), cim_ref[...]) + d_ref[...] * u
    y_ref[...] = _gelu_tanh(y)


def _s5_scan(zb, b_big_re, b_big_im, c_big_re, c_big_im, d_skip, pw_re, pw_im, batch, seq):
    w = S5_SLAB_STATE
    return pl.pallas_call(
        _s5_scan_kernel,
        grid=(batch, N_S5_SLABS),
        in_specs=[
            pl.BlockSpec((seq, S5_SLAB), lambda b, j: (b, COL_UB // S5_SLAB + j)),
            pl.BlockSpec((None, S5_SLAB, w), lambda b, j: (j, 0, 0)),
            pl.BlockSpec((None, S5_SLAB, w), lambda b, j: (j, 0, 0)),
            pl.BlockSpec((None, w, S5_SLAB), lambda b, j: (j, 0, 0)),
            pl.BlockSpec((None, w, S5_SLAB), lambda b, j: (j, 0, 0)),
            pl.BlockSpec((1, S5_SLAB), lambda b, j: (0, j)),
            pl.BlockSpec((SUBLANES, w), lambda b, j: (0, j)),
            pl.BlockSpec((SUBLANES, w), lambda b, j: (0, j)),
        ],
        out_specs=pl.BlockSpec((seq, S5_SLAB), lambda b, j: (b, j)),
        out_shape=jax.ShapeDtypeStruct((batch * seq, S5_WIDTH), jnp.float32),
        scratch_shapes=[pltpu.VMEM((w // LANES, seq, LANES), jnp.float32)] * 2,
        compiler_params=_cparams("parallel", "arbitrary"),
        name="s5_scan",
    )(zb, b_big_re, b_big_im, c_big_re, c_big_im, d_skip.reshape(1, S5_WIDTH), pw_re, pw_im)


def _s5_glu_kernel(y_ref, w_ref, g_ref, o_ref):
    y = y_ref[...]
    o = y * _sigmoid(_dot(_bf16(y), w_ref[...]))
    o_ref[...] = _bf16(o * _rms_scale(o) * g_ref[...])


def _s5_glu(y, w_glu, g_out, tm=ROW_TILE):
    t, d = y.shape
    return pl.pallas_call(
        _s5_glu_kernel,
        grid=(t // tm,),
        in_specs=[pl.BlockSpec((tm, d), lambda i: (i, 0)), pl.BlockSpec((d, d), lambda i: (0, 0)),
                  pl.BlockSpec((1, d), lambda i: (0, 0))],
        out_specs=pl.BlockSpec((tm, d), lambda i: (i, 0)),
        out_shape=jax.ShapeDtypeStruct((t, d), jnp.bfloat16),
        compiler_params=_cparams("parallel"),
        name="s5_glu_norm",
    )(y, w_glu, g_out.reshape(1, d))


def _t5_bucket(dist):
    n = jnp.maximum(dist, 0)
    nf = jnp.maximum(n, 1).astype(jnp.float32)
    large = REL_EXACT + (jnp.log(nf / REL_EXACT) / math.log(REL_MAX_DIST / REL_EXACT)
                         * (REL_BUCKETS - REL_EXACT)).astype(jnp.int32)
    return jnp.where(n < REL_EXACT, n, jnp.minimum(large, REL_BUCKETS - 1))


def _lookup(rel_ref, bucket, h):
    out = jnp.zeros(bucket.shape, jnp.float32)
    for b in range(REL_BUCKETS):
        out = jnp.where(bucket == b, rel_ref[b, h] * LOG2E, out)
    return out


def _bias_cmp_kernel(rel_ref, o_ref):
    qi = pl.program_id(0)
    n = lax.broadcasted_iota(jnp.int32, (N_CMP_PAD, TQ), 0)
    t = qi * TQ + lax.broadcasted_iota(jnp.int32, (N_CMP_PAD, TQ), 1)
    dist = t - (n * CMP_STRIDE + CMP_BLOCK - 1)
    valid = (dist >= 0) & (n < N_CMP_PAD - 1)
    bucket = _t5_bucket(dist)
    for h in range(NSA_HEADS):
        o_ref[h] = jnp.where(valid, _lookup(rel_ref, bucket, h), NEG_INF)


def _bias_tile_kernel(rel_ref, o_ref):
    j = lax.broadcasted_iota(jnp.int32, (TQ, TQ), 0)
    i = lax.broadcasted_iota(jnp.int32, (TQ, TQ), 1)
    b0 = _t5_bucket(i - j)
    b1 = _t5_bucket(i - j + TQ)
    for h in range(NSA_HEADS):
        far = jnp.full((TQ, TQ), rel_ref[REL_BUCKETS - 1, h] * LOG2E, jnp.float32)
        o_ref[h, 0] = jnp.where(i >= j, _lookup(rel_ref, b0, h), NEG_INF)
        o_ref[h, 1] = _lookup(rel_ref, b1, h)
        o_ref[h, BIAS_TILE_FAR] = far
        o_ref[h, BIAS_TILE_EDGE] = jnp.where(j > i, far, NEG_INF)
        o_ref[h, BIAS_TILE_MASKED] = jnp.full((TQ, TQ), NEG_INF, jnp.float32)


def _nsa_bias_tables(rel_bias, seq):
    smem = pl.BlockSpec(memory_space=pltpu.SMEM)
    nq = seq // TQ
    bias_c = pl.pallas_call(
        _bias_cmp_kernel,
        grid=(nq,),
        in_specs=[smem],
        out_specs=pl.BlockSpec((NSA_HEADS, None, N_CMP_PAD, TQ), lambda i: (0, i, 0, 0)),
        out_shape=jax.ShapeDtypeStruct((NSA_HEADS, nq, N_CMP_PAD, TQ), jnp.float32),
        compiler_params=_cparams("parallel"),
        name="nsa_bias_cmp",
    )(rel_bias)
    bias_t = pl.pallas_call(
        _bias_tile_kernel,
        in_specs=[smem],
        out_specs=pl.BlockSpec((NSA_HEADS, N_BIAS_TILES, TQ, TQ), lambda: (0, 0, 0, 0)),
        out_shape=jax.ShapeDtypeStruct((NSA_HEADS, N_BIAS_TILES, TQ, TQ), jnp.float32),
        compiler_params=pltpu.CompilerParams(vmem_limit_bytes=VMEM_LIMIT),
        name="nsa_bias_tiles",
    )(rel_bias)
    return bias_c, bias_t


def _nsa_compress_kernel(transpose_out, x_ref, pe_ref, w_ref, o_ref):
    half = CMP_STRIDE * HEAD_DIM
    x = x_ref[...]
    pe = pe_ref[...]
    top = _dot(_bf16(x + pe[:, :half]), w_ref[:half, :])
    bot = _dot(_bf16(x + pe[:, half:]), w_ref[half:, :])
    n = x.shape[0]
    out = top + pltpu.roll(bot, n - 1, 0)
    o_ref[...] = _bf16(out.T if transpose_out else out)


def _compress_chunks(zb, batch, seq):
    nchunk = seq // CMP_STRIDE
    heads = 2 * NSA_KV_HEADS
    x = zb[:, COL_KCC:COL_KCC + heads * HEAD_DIM].reshape(batch, nchunk, CMP_STRIDE, heads, HEAD_DIM)
    return x.transpose(0, 3, 1, 2, 4).reshape(batch, heads, nchunk, CMP_STRIDE * HEAD_DIM)


def _nsa_compress(x, head0, pe, w, transpose_out):
    batch, _, nchunk, half = x.shape
    return pl.pallas_call(
        functools.partial(_nsa_compress_kernel, transpose_out),
        grid=(batch, NSA_KV_HEADS),
        in_specs=[pl.BlockSpec((None, None, nchunk, half), lambda b, g: (b, head0 + g, 0, 0)),
                  pl.BlockSpec((1, 2 * half), lambda b, g: (0, 0)),
                  pl.BlockSpec((2 * half, HEAD_DIM), lambda b, g: (0, 0))],
        out_specs=pl.BlockSpec((None, None, nchunk, HEAD_DIM), lambda b, g: (b, g, 0, 0)),
        out_shape=jax.ShapeDtypeStruct((batch, NSA_KV_HEADS, nchunk, HEAD_DIM), jnp.bfloat16),
        compiler_params=_cparams("parallel", "parallel"),
        name="nsa_compress",
    )(x, pe.reshape(1, 2 * half), _bf16(w.reshape(2 * half, HEAD_DIM)))


def _nsa_attn_kernel(q_ref, kc_ref, vct_ref, ks_ref, vst_ref, kw_ref, vwt_ref, zg_ref, bc_ref, bt_ref,
                     ovl_ref, exp_ref, o_ref, qt_ref, acc_ref, ocmp_ref, oslc_ref):
    qi = pl.program_id(2)
    r4 = NSA_REP
    q = q_ref[...].astype(jnp.float32)
    for r in range(r4):
        qt_ref[:, r * TQ:(r + 1) * TQ] = _bf16(q[:, r * HEAD_DIM:(r + 1) * HEAD_DIM].T)
    qt = qt_ref[...]

    st = _dot(kc_ref[...], qt)
    psum = jnp.zeros((N_CMP_PAD, TQ), jnp.float32)
    ps = []
    for r in range(r4):
        bias = bc_ref[r]
        valid = bias > 0.5 * NEG_INF
        s = jnp.where(valid, st[:, r * TQ:(r + 1) * TQ] * QK_SCALE + bias, NEG_INF)
        e = jnp.exp2(s - jnp.max(s, axis=0, keepdims=True))
        p = jnp.where(valid, e * (1.0 / jnp.sum(e, axis=0, keepdims=True)), 0.0)
        psum = psum + p
        ps.append(_bf16(p))
    ocmp_ref[...] = _dot(vct_ref[...], jnp.concatenate(ps, axis=1))

    imp = _dot(ovl_ref[...], _bf16(psum))
    jb = lax.broadcasted_iota(jnp.int32, (N_SLC, TQ), 0)
    tq = qi * TQ + lax.broadcasted_iota(jnp.int32, (N_SLC, TQ), 1)
    cur = tq // SLC_BLOCK
    forced = ((jb == 0) | (jb == cur) | (jb == cur - 1)).astype(jnp.float32)
    score = jnp.where(jb * SLC_BLOCK <= tq, imp + FORCE_BONUS * forced, NEG_INF)
    rank = jnp.zeros((N_SLC, TQ), jnp.float32)
    for i in range(N_SLC):
        si = score[i:i + 1]
        beats = (si > score) | ((si == score) & (jb > i))
        rank = rank + beats.astype(jnp.float32)
    sel = _bf16((rank < SLC_TOPN).astype(jnp.float32))

    def logits(k_ref, first_tile, n_tiles, tile_of_dist):
        off = pl.multiple_of(first_tile * TQ, TQ)
        st = _dot(k_ref[pl.ds(off, n_tiles * TQ), :], qt)
        tiles = [tile_of_dist(qi - (first_tile + u)) for u in range(n_tiles)]
        out = []
        for r in range(r4):
            bias = jnp.concatenate([bt_ref[r, tl] for tl in tiles], axis=0)
            out.append(st[:, r * TQ:(r + 1) * TQ] * QK_SCALE + bias)
        return off, out

    slc_tile = lambda d: jnp.where(d < 0, BIAS_TILE_MASKED, jnp.minimum(d, BIAS_TILE_FAR))
    keys = SLC_GROUP * TQ
    acc_ref[...] = jnp.zeros_like(acc_ref)

    def slc_body(it, carry):
        m, l = carry
        off, chunks = logits(ks_ref, it * SLC_GROUP, SLC_GROUP, slc_tile)
        selb = (_dot(exp_ref[pl.ds(off, keys), :], sel) - 1.0) * (-NEG_INF)
        p, alpha, m, l = _softmax_step([s + selb for s in chunks], m, l, keys)
        acc_ref[...] = acc_ref[...] * alpha + _dot(vst_ref[:, pl.ds(off, keys)], p)
        return m, l

    init = (jnp.full((1, r4 * TQ), -jnp.inf, jnp.float32), jnp.zeros((1, r4 * TQ), jnp.float32))
    _, l = lax.fori_loop(0, qi // SLC_GROUP + 1, slc_body, init)
    oslc_ref[...] = acc_ref[...] * (1.0 / l)

    far_d = WINDOW // TQ
    win_tile = lambda d: jnp.where(d < 0, BIAS_TILE_MASKED, jnp.where(
        d < BIAS_TILE_FAR, d, jnp.where(d < far_d, BIAS_TILE_FAR, BIAS_TILE_EDGE)))
    off, chunks = logits(kw_ref, jnp.maximum(qi - far_d, 0), WIN_TILES, win_tile)
    p, _, _, l = _softmax_step(chunks, init[0], init[1], WIN_TILES * TQ)
    acc_ref[...] = _dot(vwt_ref[:, pl.ds(off, WIN_TILES * TQ)], p) * (1.0 / l)

    gt = _sigmoid(zg_ref[...]).T
    for r in range(r4):
        lanes = slice(r * TQ, (r + 1) * TQ)
        o = (gt[3 * r:3 * r + 1] * ocmp_ref[:, lanes] + gt[3 * r + 1:3 * r + 2] * oslc_ref[:, lanes]
             + gt[3 * r + 2:3 * r + 3] * acc_ref[:, lanes])
        o_ref[:, r * HEAD_DIM:(r + 1) * HEAD_DIM] = o.T


def _nsa_consts(seq):
    nc = np.arange(N_CMP_PAD)
    blk_start = nc * CMP_STRIDE
    blk_end = blk_start + CMP_BLOCK - 1
    sel_start = np.arange(N_SLC) * SLC_BLOCK
    ovl = ((blk_start[None, :] <= sel_start[:, None] + SLC_BLOCK - 1) & (blk_end[None, :] >= sel_start[:, None])
           & (nc[None, :] < N_CMP_PAD - 1))
    expand = (np.arange(seq)[:, None] // SLC_BLOCK) == np.arange(N_SLC)[None, :]
    return jnp.asarray(ovl, jnp.bfloat16), jnp.asarray(expand, jnp.bfloat16)


def _nsa_attn(za, vt, z_gate, kc, vct, bias_c, bias_t, batch, seq):
    nq = seq // TQ
    ovl, expand = _nsa_consts(seq)
    cb = lambda c: c // HEAD_DIM
    keys = lambda col: pl.BlockSpec((seq, HEAD_DIM), lambda b, g, i: (b, cb(col) + g))
    vals = lambda h0: pl.BlockSpec((None, None, HEAD_DIM, seq), lambda b, g, i: (b, h0 + g, 0, 0))
    cmp = pl.BlockSpec((None, None, N_CMP_PAD, HEAD_DIM), lambda b, g, i: (b, g, 0, 0))
    qw = NSA_REP * HEAD_DIM
    wide = pltpu.VMEM((HEAD_DIM, NSA_REP * TQ), jnp.float32)
    return pl.pallas_call(
        _nsa_attn_kernel,
        grid=(batch, NSA_KV_HEADS, nq),
        in_specs=[
            pl.BlockSpec((TQ, qw), lambda b, g, i: (b * nq + i, COL_QC // qw + g)),
            cmp, cmp, keys(COL_KCS), vals(V_HEAD_SLC), keys(COL_KCW), vals(V_HEAD_WIN),
            pl.BlockSpec((TQ, LANES), lambda b, g, i: (b * nq + i, COL_GATE // LANES + 1 + g)),
            pl.BlockSpec((NSA_REP, None, N_CMP_PAD, TQ), lambda b, g, i: (g, i, 0, 0)),
            pl.BlockSpec((NSA_REP, N_BIAS_TILES, TQ, TQ), lambda b, g, i: (g, 0, 0, 0)),
            pl.BlockSpec((N_SLC, N_CMP_PAD), lambda b, g, i: (0, 0)),
            pl.BlockSpec((seq, N_SLC), lambda b, g, i: (0, 0)),
        ],
        out_specs=pl.BlockSpec((TQ, qw), lambda b, g, i: (b * nq + i, g)),
        out_shape=jax.ShapeDtypeStruct((batch * seq, NSA_WIDTH), jnp.float32),
        scratch_shapes=[pltpu.VMEM((HEAD_DIM, NSA_REP * TQ), jnp.bfloat16), wide, wide, wide],
        compiler_params=_cparams("parallel", "parallel", "arbitrary"),
        name="nsa_attention",
    )(za, kc, vct, za, vt, za, vt, z_gate, bias_c, bias_t, ovl, expand)


def _mixnorm_kernel(oa_ref, ob_ref, oc_ref, ga_ref, gc_ref, o_ref):
    oa = oa_ref[...]
    oc = oc_ref[...]
    o_ref[:, :FOX_WIDTH] = _bf16(oa * _rms_scale(oa) * ga_ref[...])
    o_ref[:, FOX_WIDTH:FOX_WIDTH + S5_WIDTH] = ob_ref[...]
    o_ref[:, FOX_WIDTH + S5_WIDTH:] = _bf16(oc * _rms_scale(oc) * gc_ref[...])


def _mixnorm(o_a, o_b, o_c, g_a, g_c, tm=ROW_TILE):
    t = o_a.shape[0]
    row = lambda w: pl.BlockSpec((tm, w), lambda i: (i, 0))
    vec = lambda w: pl.BlockSpec((1, w), lambda i: (0, 0))
    return pl.pallas_call(
        _mixnorm_kernel,
        grid=(t // tm,),
        in_specs=[row(FOX_WIDTH), row(S5_WIDTH), row(NSA_WIDTH), vec(FOX_WIDTH), vec(NSA_WIDTH)],
        out_specs=row(D_MODEL),
        out_shape=jax.ShapeDtypeStruct((t, D_MODEL), jnp.bfloat16),
        compiler_params=_cparams("parallel"),
        name="mixer_norm_concat",
    )(o_a, o_b, o_c, g_a.reshape(1, -1), g_c.reshape(1, -1))


def _split_w_in(w):
    f, kv = FOX_WIDTH, NSA_KV_WIDTH
    o = 0
    qa, o = w[:, o:o + f], o + f
    ka, o = w[:, o:o + f], o + f
    va, o = w[:, o:o + f], o + f
    zf, o = w[:, o:o + FOX_HEADS], o + FOX_HEADS
    ub, o = w[:, o:o + S5_WIDTH], o + S5_WIDTH
    qc, o = w[:, o:o + NSA_WIDTH], o + NSA_WIDTH
    kcc_vcc, o = w[:, o:o + 2 * kv], o + 2 * kv
    rest, o = w[:, o:o + 4 * kv], o + 4 * kv
    zg = w[:, o:o + 3 * NSA_HEADS]
    kcs, vcs, kcw, vcw = (rest[:, n * kv:(n + 1) * kv] for n in range(4))
    w_attn = _bf16(jnp.concatenate([qa, ka, qc, kcs, kcw, va, vcs, vcw], axis=1))
    per_g = 3 * NSA_REP
    gates = [jnp.pad(zf, ((0, 0), (0, LANES - FOX_HEADS)))]
    for g in range(NSA_KV_HEADS):
        gates.append(jnp.pad(zg[:, g * per_g:(g + 1) * per_g], ((0, 0), (0, LANES - per_g))))
    w_f32 = _bf16(jnp.concatenate([ub, kcc_vcc] + gates, axis=1))
    return w_attn, w_f32


def _block_diag(blocks):
    j, n, a, b = blocks.shape
    eye = jnp.eye(n, dtype=blocks.dtype)
    return (blocks[:, :, :, None, :] * eye[None, :, None, :, None]).reshape(j, n * a, n * b)


def _s5_weights(bb_re, bb_im, c_re, c_im):
    n = S5_SLAB // S5_GROUP_CH
    layers = bb_re.shape[0]

    def slabs(m, a, b):
        d = _block_diag(m.reshape(layers * N_S5_SLABS, n, a, b))
        return _bf16(d.reshape(layers, N_S5_SLABS, n * a, n * b))

    inp = lambda m: slabs(m.transpose(0, 1, 3, 2), S5_GROUP_CH, S5_STATE)
    out = lambda m: slabs(m.transpose(0, 1, 3, 2), S5_STATE, S5_GROUP_CH)
    return inp(bb_re), inp(bb_im), out(c_re), out(c_im)


def kernel(x, w_in, b_forget, s5_a_re, s5_a_im, s5_log_dt, s5_b_re, s5_b_im, s5_c_re, s5_c_im, s5_d, s5_w_glu,
           cmp_pe_k, cmp_pe_v, cmp_w_k, cmp_w_v, rel_bias, g_out_fox, g_out_s5, g_out_nsa, w_out, g_pre_mix,
           g_post_mix, g_pre_ffn, g_post_ffn, w_up, conv_w, conv_b, w_down):
    batch, seq, d = x.shape
    t = batch * seq
    xf = x.reshape(t, d)

    bias_c, bias_t = _nsa_bias_tables(rel_bias, seq)
    bb_re, bb_im, pw_re, pw_im = _s5_discretize(s5_a_re, s5_a_im, s5_log_dt, s5_b_re, s5_b_im)
    s5w = _s5_weights(bb_re, bb_im, s5_c_re, s5_c_im)

    hn = _rmsnorm_cast(xf, g_pre_mix[0])
    for l in range(DEPTH):
        w_attn, w_f32 = _split_w_in(w_in[l])
        za = _matmul(hn, w_attn, IN_PROJ_TN, jnp.bfloat16, "in_proj_attn")
        zb = _matmul(hn, w_f32, N_F32 // 3, jnp.float32, "in_proj_f32")

        ccol, crow = _fox_gate(zb, b_forget[l], batch, seq)
        vt = _values_t(za, batch, seq)
        o_a = _fox_attn(za, vt, ccol, crow, batch, seq)

        y_b = _s5_scan(zb, *(w[l] for w in s5w), s5_d[l], pw_re[l], pw_im[l], batch, seq)
        o_b = _s5_glu(y_b, _bf16(s5_w_glu[l]), g_out_s5[l])

        chunks = _compress_chunks(zb, batch, seq)
        kc = _nsa_compress(chunks, 0, cmp_pe_k[l], cmp_w_k[l], False)
        vct = _nsa_compress(chunks, NSA_KV_HEADS, cmp_pe_v[l], cmp_w_v[l], True)
        o_c = _nsa_attn(za, vt, zb, kc, vct, bias_c, bias_t, batch, seq)

        mixed = _mixnorm(o_a, o_b, o_c, g_out_fox[l], g_out_nsa[l])
        xf, hn = _matmul_resid_cols(mixed, _bf16(w_out[l]), xf, g_post_mix[l], g_pre_ffn[l], "out_proj")

        wg, wv = _bf16(w_up[l][:, :D_FF]), _bf16(w_up[l][:, D_FF:])
        cwg, cwv = conv_w[l][:, :D_FF], conv_w[l][:, D_FF:]
        cbg, cbv = conv_b[l][None, :D_FF], conv_b[l][None, D_FF:]
        act = _ffn_up(hn, wg, wv, cwg, cwv, cbg, cbv, seq)
        g_next = g_pre_mix[l + 1] if l + 1 < DEPTH else jnp.ones((d,), jnp.float32)
        xf, hn = _matmul_resid(act, _bf16(w_down[l]), xf, g_post_ffn[l], g_next, "ffn_down")
    return xf.reshape(batch, seq, d)
```
